```python
import math
import jax
import jax.numpy as jnp
from jax import lax
import numpy as np

D_MODEL = 4096
BATCH = 4
SEQ = 4096
DEPTH = 1

D_RWKV = D_MODEL // 2
RWKV_HEAD = 64
RWKV_HEADS = D_RWKV // RWKV_HEAD
W_LORA = 128
A_LORA = 128
G_LORA = 512
RWKV_GN_EPS = 64e-5
D_S5 = D_MODEL // 2
S5_GROUP = 16
S5_GROUPS = D_S5 // S5_GROUP
S5_STATE = 64
DT_MIN = 1e-3
DT_MAX = 1e-1
N_EXPERTS = 32
TOP_K = 4
D_EXPERT = D_MODEL // 4
SWIGLU_LIMIT = 7.0
SWIGLU_ALPHA = 1.702
EXPERT_BLOCK = 256
ALPHA = (2 * DEPTH) ** 0.25
BETA = (8 * DEPTH) ** -0.25
LN_EPS = 1e-5
SHIFT_WIDTH = 3 * D_RWKV + W_LORA + A_LORA + G_LORA
IN_WIDTH = SHIFT_WIDTH + D_S5 + 2 * D_MODEL

kernel_name = 'hybrid_rwkv7_s5_moe_block'


def layer_norm(x, gain=None, bias=None):
    xf = x.astype(jnp.float32)
    mean = jnp.mean(xf, axis=-1, keepdims=True)
    var = jnp.mean(jnp.square(xf - mean), axis=-1, keepdims=True)
    y = (xf - mean) * lax.rsqrt(var + LN_EPS)
    if gain is not None:
        y = y * gain.astype(jnp.float32) + bias.astype(jnp.float32)
    return y.astype(x.dtype)


def token_shift(p):
    return jnp.pad(p, ((0, 0), (1, 0), (0, 0)))[:, :-1]


def rwkv7_time_mix(p, mu, w0, w2, a0, a2, g2, k_k, k_a, r_k, ln_g, ln_b, w_proj):
    f32 = jnp.float32
    bsz, seq, _ = p.shape
    p = p.astype(f32)
    p = p + (token_shift(p) - p) * mu.astype(f32)
    cuts = [D_RWKV, 2 * D_RWKV, 3 * D_RWKV, 3 * D_RWKV + W_LORA, 3 * D_RWKV + W_LORA + A_LORA]
    r, k, v, xw, xa, xg = jnp.split(p, cuts, axis=-1)
    w = -jax.nn.softplus(-(w0.astype(f32) + jnp.tanh(xw) @ w2.astype(f32))) - 0.5
    decay = jnp.exp(-jnp.exp(w))
    a = jax.nn.sigmoid(a0.astype(f32) + xa @ a2.astype(f32))
    g = jax.nn.sigmoid(xg) @ g2.astype(f32)
    heads = lambda t: t.reshape(bsz, seq, RWKV_HEADS, RWKV_HEAD)
    r, k, v, a, decay = heads(r), heads(k), heads(v), heads(a), heads(decay)
    kk = k * k_k.astype(f32)
    kk = kk / jnp.maximum(jnp.sqrt(jnp.sum(kk * kk, axis=-1, keepdims=True)), 1e-12)
    k = k * (1.0 + (a - 1.0) * k_a.astype(f32))

    def step(state, inp):
        r_t, w_t, k_t, v_t, kk_t, a_t = inp
        sa = jnp.einsum('bhvk,bhk->bhv', state, -kk_t)
        state = (state * w_t[:, :, None, :]
                 + sa[..., None] * (kk_t * a_t)[:, :, None, :]
                 + v_t[..., None] * k_t[:, :, None, :])
        return state, jnp.einsum('bhvk,bhk->bhv', state, r_t)

    xs = tuple(jnp.swapaxes(t, 0, 1) for t in (r, decay, k, v, kk, a))
    state0 = jnp.zeros((bsz, RWKV_HEADS, RWKV_HEAD, RWKV_HEAD), f32)
    _, y = lax.scan(step, state0, xs)
    y = jnp.swapaxes(y, 0, 1)
    ym = jnp.mean(y, axis=-1, keepdims=True)
    yv = jnp.mean(jnp.square(y - ym), axis=-1, keepdims=True)
    y = (y - ym) * lax.rsqrt(yv + RWKV_GN_EPS)
    y = y * ln_g.astype(f32).reshape(RWKV_HEADS, RWKV_HEAD) + ln_b.astype(f32).reshape(RWKV_HEADS, RWKV_HEAD)
    y = y + jnp.sum(r * k * r_k.astype(f32), axis=-1, keepdims=True) * v
    y = y.reshape(bsz, seq, D_RWKV) * g
    return y @ w_proj.astype(f32)


def _ssm_combine(e1, e2):
    a1r, a1i, b1r, b1i = e1
    a2r, a2i, b2r, b2i = e2
    return (a2r * a1r - a2i * a1i,
            a2r * a1i + a2i * a1r,
            a2r * b1r - a2i * b1i + b2r,
            a2r * b1i + a2i * b1r + b2i)


def s5_ssm(u, a_re, a_im, log_dt, b_re, b_im, c_re, c_im, d, glu_w1, glu_w2):
    f32 = jnp.float32
    bsz, seq, _ = u.shape
    u = u.astype(f32).reshape(bsz, seq, S5_GROUPS, S5_GROUP)
    a_re, a_im = a_re.astype(f32), a_im.astype(f32)
    dt = jnp.exp(log_dt.astype(f32))[:, None]
    mag = jnp.exp(a_re * dt)
    ang = a_im * dt
    ab_re, ab_im = mag * jnp.cos(ang), mag * jnp.sin(ang)
    den = a_re * a_re + a_im * a_im
    f_re = ((ab_re - 1.0) * a_re + ab_im * a_im) / den
    f_im = (ab_im * a_re - (ab_re - 1.0) * a_im) / den
    b_re, b_im = b_re.astype(f32), b_im.astype(f32)
    bb_re = f_re[..., None] * b_re - f_im[..., None] * b_im
    bb_im = f_re[..., None] * b_im + f_im[..., None] * b_re
    bu_re = jnp.einsum('gph,blgh->blgp', bb_re, u)
    bu_im = jnp.einsum('gph,blgh->blgp', bb_im, u)
    a_seq_re = jnp.broadcast_to(ab_re, (1, seq, S5_GROUPS, S5_STATE))
    a_seq_im = jnp.broadcast_to(ab_im, (1, seq, S5_GROUPS, S5_STATE))
    _, _, s_re, s_im = lax.associative_scan(_ssm_combine, (a_seq_re, a_seq_im, bu_re, bu_im), axis=1)
    y = (jnp.einsum('ghp,blgp->blgh', c_re.astype(f32), s_re)
         - jnp.einsum('ghp,blgp->blgh', c_im.astype(f32), s_im)
         + d.astype(f32) * u)
    z = jax.nn.gelu(y.reshape(bsz, seq, D_S5))
    return (z @ glu_w1.astype(f32)) * jax.nn.sigmoid(z @ glu_w2.astype(f32))


def moe_ffn(h, router_w, router_b, w_gate, b_gate, w_up, b_up, w_down, b_down):
    n_tok, d = h.shape
    logits = h.astype(jnp.float32) @ router_w.astype(jnp.float32) + router_b.astype(jnp.float32)
    top_logits, top_idx = lax.top_k(logits, TOP_K)
    top_w = jax.nn.softmax(top_logits, axis=-1)
    n_assign = n_tok * TOP_K
    flat_e = top_idx.reshape(n_assign)
    flat_tok = jnp.arange(n_assign, dtype=jnp.int32) // TOP_K
    flat_w = top_w.reshape(n_assign)
    order = jnp.argsort(flat_e)
    sorted_e = flat_e[order]
    counts = jnp.bincount(flat_e, length=N_EXPERTS)
    padded = (counts + EXPERT_BLOCK - 1) // EXPERT_BLOCK * EXPERT_BLOCK
    start = jnp.cumsum(counts) - counts
    pad_end = jnp.cumsum(padded)
    pad_start = pad_end - padded
    dest = pad_start[sorted_e] + jnp.arange(n_assign, dtype=jnp.int32) - start[sorted_e]
    n_blocks = -(-(n_assign + N_EXPERTS * (EXPERT_BLOCK - 1)) // EXPERT_BLOCK)
    n_rows = n_blocks * EXPERT_BLOCK
    row_tok = jnp.zeros((n_rows,), jnp.int32).at[dest].set(flat_tok[order])
    row_w = jnp.zeros((n_rows,), jnp.float32).at[dest].set(flat_w[order])
    block_start = jnp.arange(n_blocks, dtype=pad_end.dtype) * EXPERT_BLOCK
    block_e = jnp.minimum(jnp.searchsorted(pad_end, block_start, side='right'), N_EXPERTS - 1)

    def expert_block(args):
        e, tok, wt = args
        xb = h[tok]
        gate = xb @ w_gate[e] + b_gate[e]
        up = xb @ w_up[e] + b_up[e]
        gate = jnp.minimum(gate, SWIGLU_LIMIT)
        up = jnp.clip(up, -SWIGLU_LIMIT, SWIGLU_LIMIT)
        act = (up + 1.0) * gate * jax.nn.sigmoid(SWIGLU_ALPHA * gate)
        y = act @ w_down[e] + b_down[e]
        return y.astype(jnp.float32) * wt[:, None]

    ys = lax.map(expert_block, (block_e, row_tok.reshape(n_blocks, EXPERT_BLOCK),
                                row_w.reshape(n_blocks, EXPERT_BLOCK)))
    return jax.ops.segment_sum(ys.reshape(n_rows, d), row_tok, num_segments=n_tok)


def setup_inputs(seed: int = 0) -> dict:
    key = jax.random.key(seed)
    ks = iter(jax.random.split(key, 48))
    f32 = jnp.float32
    nrm = lambda shape, scale: scale * jax.random.normal(next(ks), shape, f32)
    D, H, N = D_MODEL, RWKV_HEADS, RWKV_HEAD
    G, P, C = S5_GROUPS, S5_STATE, S5_GROUP
    E, F = N_EXPERTS, D_EXPERT
    x = nrm((BATCH, SEQ, D), 1.0)
    c = nrm((BATCH, D), 1.0)
    ada_w = nrm((DEPTH, D, 6 * D), 0.5 * D ** -0.5)
    ada_b = nrm((DEPTH, 6 * D), 0.01)
    w_in = nrm((DEPTH, D, IN_WIDTH), D ** -0.5)
    w_in = w_in.at[:, :, 2 * D_RWKV:3 * D_RWKV].multiply(BETA)
    shift_mu = jax.random.uniform(next(ks), (DEPTH, SHIFT_WIDTH), f32)
    ratio = jnp.arange(D_RWKV, dtype=f32) / (D_RWKV - 1)
    rwkv_w0 = (-7.0 + 5.0 * ratio ** 0.85 + 0.5)[None, :] + nrm((DEPTH, D_RWKV), 0.1)
    rwkv_w2 = nrm((DEPTH, W_LORA, D_RWKV), 0.1 * W_LORA ** -0.5)
    rwkv_a0 = nrm((DEPTH, D_RWKV), 0.1)
    rwkv_a2 = nrm((DEPTH, A_LORA, D_RWKV), 0.5 * A_LORA ** -0.5)
    rwkv_g2 = nrm((DEPTH, G_LORA, D_RWKV), G_LORA ** -0.5)
    rwkv_k_k = 0.85 + nrm((DEPTH, H, N), 0.02)
    rwkv_k_a = 1.0 + nrm((DEPTH, H, N), 0.02)
    rwkv_r_k = -0.04 + nrm((DEPTH, H, N), 0.02)
    rwkv_ln_g = 1.0 + nrm((DEPTH, D_RWKV), 0.02)
    rwkv_ln_b = nrm((DEPTH, D_RWKV), 0.02)
    rwkv_proj = nrm((DEPTH, D_RWKV, D), BETA * D_RWKV ** -0.5)
    s5_a_re = -0.5 + nrm((DEPTH, G, P), 0.01)
    s5_a_im = math.pi * jnp.arange(P, dtype=f32) + nrm((DEPTH, G, P), 0.01)
    s5_log_dt = jax.random.uniform(next(ks), (DEPTH, G), f32, math.log(DT_MIN), math.log(DT_MAX))
    s5_b_re = nrm((DEPTH, G, P, C), (2 * C) ** -0.5)
    s5_b_im = nrm((DEPTH, G, P, C), (2 * C) ** -0.5)
    s5_c_re = nrm((DEPTH, G, C, P), P ** -0.5)
    s5_c_im = nrm((DEPTH, G, C, P), P ** -0.5)
    s5_d = nrm((DEPTH, G, C), 1.0)
    s5_glu_w1 = nrm((DEPTH, D_S5, D), BETA * D_S5 ** -0.5)
    s5_glu_w2 = nrm((DEPTH, D_S5, D), D_S5 ** -0.5)
    w_out = nrm((DEPTH, D, D), BETA * D ** -0.5)
    ln1_g = 1.0 + nrm((DEPTH, D), 0.02)
    ln1_b = nrm((DEPTH, D), 0.02)
    ln2_g = 1.0 + nrm((DEPTH, D), 0.02)
    ln2_b = nrm((DEPTH, D), 0.02)
    router_w = nrm((DEPTH, D, E), D ** -0.5)
    router_b = nrm((DEPTH, E), 0.01)
    exp_w_gate = nrm((DEPTH, E, D, F), D ** -0.5)
    exp_b_gate = nrm((DEPTH, E, F), 0.01)
    exp_w_up = nrm((DEPTH, E, D, F), BETA * D ** -0.5)
    exp_b_up = nrm((DEPTH, E, F), 0.01)
    exp_w_down = nrm((DEPTH, E, F, D), BETA * F ** -0.5)
    exp_b_down = nrm((DEPTH, E, D), 0.01)
    return {'x': x, 'c': c, 'ada_w': ada_w, 'ada_b': ada_b, 'w_in': w_in, 'shift_mu': shift_mu,
            'rwkv_w0': rwkv_w0, 'rwkv_w2': rwkv_w2, 'rwkv_a0': rwkv_a0, 'rwkv_a2': rwkv_a2,
            'rwkv_g2': rwkv_g2, 'rwkv_k_k': rwkv_k_k, 'rwkv_k_a': rwkv_k_a, 'rwkv_r_k': rwkv_r_k,
            'rwkv_ln_g': rwkv_ln_g, 'rwkv_ln_b': rwkv_ln_b, 'rwkv_proj': rwkv_proj,
            's5_a_re': s5_a_re, 's5_a_im': s5_a_im, 's5_log_dt': s5_log_dt,
            's5_b_re': s5_b_re, 's5_b_im': s5_b_im, 's5_c_re': s5_c_re, 's5_c_im': s5_c_im,
            's5_d': s5_d, 's5_glu_w1': s5_glu_w1, 's5_glu_w2': s5_glu_w2, 'w_out': w_out,
            'ln1_g': ln1_g, 'ln1_b': ln1_b, 'ln2_g': ln2_g, 'ln2_b': ln2_b,
            'router_w': router_w, 'router_b': router_b,
            'exp_w_gate': exp_w_gate, 'exp_b_gate': exp_b_gate, 'exp_w_up': exp_w_up,
            'exp_b_up': exp_b_up, 'exp_w_down': exp_w_down, 'exp_b_down': exp_b_down}


def reference(x, c, ada_w, ada_b, w_in, shift_mu, rwkv_w0, rwkv_w2, rwkv_a0, rwkv_a2, rwkv_g2,
              rwkv_k_k, rwkv_k_a, rwkv_r_k, rwkv_ln_g, rwkv_ln_b, rwkv_proj,
              s5_a_re, s5_a_im, s5_log_dt, s5_b_re, s5_b_im, s5_c_re, s5_c_im, s5_d,
              s5_glu_w1, s5_glu_w2, w_out, ln1_g, ln1_b, ln2_g, ln2_b, router_w, router_b,
              exp_w_gate, exp_b_gate, exp_w_up, exp_b_up, exp_w_down, exp_b_down):
    out_dtype = x.dtype
    bsz, seq, d = x.shape
    for l in range(DEPTH):
        mod = jax.nn.silu(c) @ ada_w[l] + ada_b[l]
        sh1, sc1, g1, sh2, sc2, g2 = jnp.split(mod[:, None, :], 6, axis=-1)

        h = layer_norm(x) * (1.0 + sc1) + sh1
        p = h @ w_in[l]
        p_rwkv = p[..., :SHIFT_WIDTH]
        p_s5 = p[..., SHIFT_WIDTH:SHIFT_WIDTH + D_S5]
        gate_rwkv = p[..., SHIFT_WIDTH + D_S5:SHIFT_WIDTH + D_S5 + d]
        gate_s5 = p[..., SHIFT_WIDTH + D_S5 + d:]
        y_rwkv = rwkv7_time_mix(p_rwkv, shift_mu[l], rwkv_w0[l], rwkv_w2[l], rwkv_a0[l], rwkv_a2[l],
                                rwkv_g2[l], rwkv_k_k[l], rwkv_k_a[l], rwkv_r_k[l],
                                rwkv_ln_g[l], rwkv_ln_b[l], rwkv_proj[l])
        y_s5 = s5_ssm(p_s5, s5_a_re[l], s5_a_im[l], s5_log_dt[l], s5_b_re[l], s5_b_im[l],
                      s5_c_re[l], s5_c_im[l], s5_d[l], s5_glu_w1[l], s5_glu_w2[l])
        merged = jax.nn.sigmoid(gate_rwkv) * y_rwkv + jax.nn.sigmoid(gate_s5) * y_s5
        mix = merged @ w_out[l]
        x = layer_norm(ALPHA * x + g1 * mix, ln1_g[l], ln1_b[l])

        h2 = layer_norm(x) * (1.0 + sc2) + sh2
        ff = moe_ffn(h2.reshape(bsz * seq, d), router_w[l], router_b[l], exp_w_gate[l], exp_b_gate[l],
                     exp_w_up[l], exp_b_up[l], exp_w_down[l], exp_b_down[l]).reshape(bsz, seq, d)
        x = layer_norm(ALPHA * x + g2 * ff, ln2_g[l], ln2_b[l])
    return x.astype(out_dtype)
```

```python
import functools

import jax
import jax.numpy as jnp
from jax import lax
from jax.experimental import pallas as pl
from jax.experimental.pallas import tpu as pltpu

F32 = jnp.float32
BF16 = jnp.bfloat16
I32 = jnp.int32

LN_EPS = 1e-5
RWKV_GN_EPS = 64e-5
TOP_K = 4
EXPERT_BLOCK = 256
SWIGLU_LIMIT = 7.0
SWIGLU_ALPHA = 1.702
RWKV_CHUNK = 64
S5_CHUNK = 64
LANES = 128
VMEM_LIMIT = 56 * 1024 * 1024
HI = lax.Precision.HIGHEST


def _cp(sem, vmem=VMEM_LIMIT):
    return pltpu.CompilerParams(dimension_semantics=sem, vmem_limit_bytes=vmem)


def _dot(a, b):
    return jnp.dot(a.astype(BF16), b.astype(BF16), preferred_element_type=F32)


def _dot_nt(a, b, precision=None):
    return lax.dot_general(a, b, (((1,), (1,)), ((), ())), preferred_element_type=F32,
                           precision=precision)


def _dot_tn(a, b):
    return lax.dot_general(a.astype(BF16), b.astype(BF16), (((0,), (0,)), ((), ())),
                           preferred_element_type=F32)


def _ln(x):
    mean = jnp.mean(x, axis=-1, keepdims=True)
    xc = x - mean
    var = jnp.mean(xc * xc, axis=-1, keepdims=True)
    return xc * lax.rsqrt(var + LN_EPS)


def _sigmoid(x):
    return 1.0 / (1.0 + jnp.exp(-x))


def _tile(n, pref, align):
    if n <= pref:
        return n
    t = pref - pref % align
    while n % t:
        t -= align
    return t


def _mm_body(a_ref, b_ref, bias_ref, o_ref, *, silu_a):
    a = a_ref[...]
    if silu_a:
        a = a * _sigmoid(a)
    o_ref[...] = (_dot(a, b_ref[...]) + bias_ref[...]).astype(o_ref.dtype)


def _matmul(a, b, bias=None, *, out_dtype, tm, tn, silu_a=False):
    m, k = a.shape
    n = b.shape[1]
    tm, tn = _tile(m, tm, 8), _tile(n, tn, LANES)
    if bias is None:
        bias = jnp.zeros((1, n), F32)
    return pl.pallas_call(
        functools.partial(_mm_body, silu_a=silu_a),
        grid=(m // tm, n // tn),
        in_specs=[pl.BlockSpec((tm, k), lambda i, j: (i, 0)),
                  pl.BlockSpec((k, tn), lambda i, j: (0, j)),
                  pl.BlockSpec((1, tn), lambda i, j: (0, j))],
        out_specs=pl.BlockSpec((tm, tn), lambda i, j: (i, j)),
        out_shape=jax.ShapeDtypeStruct((m, n), out_dtype),
        compiler_params=_cp(("parallel", "parallel")),
    )(a, b, bias)


def _ln_mod_body(x_ref, sc_ref, sh_ref, o_ref):
    y = _ln(x_ref[0])
    o_ref[0] = (y * (1.0 + sc_ref[0]) + sh_ref[0]).astype(o_ref.dtype)


def _ln_mod(x, mod3, sc_idx, sh_idx, *, tl=256):
    bsz, seq, d = x.shape
    tl = min(tl, seq)
    return pl.pallas_call(
        _ln_mod_body,
        grid=(bsz, seq // tl),
        in_specs=[pl.BlockSpec((1, tl, d), lambda b, l: (b, l, 0)),
                  pl.BlockSpec((1, 1, d), lambda b, l: (b * 6 + sc_idx, 0, 0)),
                  pl.BlockSpec((1, 1, d), lambda b, l: (b * 6 + sh_idx, 0, 0))],
        out_specs=pl.BlockSpec((1, tl, d), lambda b, l: (b, l, 0)),
        out_shape=jax.ShapeDtypeStruct((bsz, seq, d), BF16),
        compiler_params=_cp(("parallel", "parallel")),
    )(x, mod3, mod3)


def _token_shift(x, carry_ref, mu):
    rolled = pltpu.roll(x, 1, 0)
    row = lax.broadcasted_iota(I32, x.shape, 0)
    prev = jnp.where(row == 0, carry_ref[...], rolled)
    carry_ref[...] = x[x.shape[0] - 1:, :]
    return x + (prev - x) * mu


def _rwkv_prep_body(prkv_ref, plora_ref, mu_rkv_ref, mu_lora_ref, w0_ref, w2_ref, a0_ref, a2_ref,
                    g2_ref, r_ref, k_ref, v_ref, a_ref, lw_ref, g_ref, carry_rkv, carry_lora,
                    *, dr, w_lora, a_lora):
    @pl.when(pl.program_id(1) == 0)
    def _():
        carry_rkv[...] = jnp.zeros_like(carry_rkv)
        carry_lora[...] = jnp.zeros_like(carry_lora)

    xs = _token_shift(prkv_ref[0].astype(F32), carry_rkv, mu_rkv_ref[...])
    r_ref[0] = xs[:, :dr].astype(r_ref.dtype)
    k_ref[0] = xs[:, dr:2 * dr].astype(k_ref.dtype)
    v_ref[0] = xs[:, 2 * dr:].astype(v_ref.dtype)

    ls = _token_shift(plora_ref[0].astype(F32), carry_lora, mu_lora_ref[...])
    xw = ls[:, :w_lora]
    xa = ls[:, w_lora:w_lora + a_lora]
    xg = ls[:, w_lora + a_lora:]
    z = -(w0_ref[...] + _dot(jnp.tanh(xw), w2_ref[...]))
    softplus = jnp.maximum(z, 0.0) + jnp.log1p(jnp.exp(-jnp.abs(z)))
    lw_ref[0] = -jnp.exp(-softplus - 0.5)
    a_ref[0] = _sigmoid(a0_ref[...] + _dot(xa, a2_ref[...])).astype(a_ref.dtype)
    g_ref[0] = _dot(_sigmoid(xg), g2_ref[...]).astype(g_ref.dtype)


def _rwkv_prep(p_rkv, p_lora, mu, w0, w2, a0, a2, g2, *, tl=256):
    bsz, seq, dr3 = p_rkv.shape
    dr = dr3 // 3
    lora = p_lora.shape[-1]
    w_lora, a_lora = w2.shape[0], a2.shape[0]
    tl = min(tl, seq)
    tok = lambda w: pl.BlockSpec((1, tl, w), lambda b, l: (b, l, 0))
    full = lambda a: pl.BlockSpec(a.shape, lambda b, l: (0,) * a.ndim)
    mu_rkv, mu_lora = mu[None, :dr3], mu[None, dr3:]
    args = (p_rkv, p_lora, mu_rkv, mu_lora, w0[None], w2.astype(BF16), a0[None], a2.astype(BF16),
            g2.astype(BF16))
    out = jax.ShapeDtypeStruct((bsz, seq, dr), BF16)
    return pl.pallas_call(
        functools.partial(_rwkv_prep_body, dr=dr, w_lora=w_lora, a_lora=a_lora),
        grid=(bsz, seq // tl),
        in_specs=[tok(dr3), tok(lora)] + [full(a) for a in args[2:]],
        out_specs=[tok(dr)] * 6,
        out_shape=[out, out, out, out, jax.ShapeDtypeStruct((bsz, seq, dr), F32), out],
        scratch_shapes=[pltpu.VMEM((1, dr3), F32), pltpu.VMEM((1, lora), F32)],
        compiler_params=_cp(("parallel", "arbitrary")),
    )(*args)


def _rwkv_rec_body(r_ref, k_ref, v_ref, a_ref, lw_ref, g_ref, kk_ref, ka_ref, rk_ref, lng_ref,
                   lnb_ref, y_ref, state, *, heads, n):
    c = r_ref.shape[1]

    @pl.when(pl.program_id(2) == 0)
    def _():
        state[...] = jnp.zeros_like(state)

    row = lax.broadcasted_iota(I32, (c, c), 0)
    col = lax.broadcasted_iota(I32, (c, c), 1)
    lower = row >= col
    strict = row > col
    tril_ones = lower.astype(F32)
    eye = (row == col).astype(F32)

    for h in range(heads):
        sl = slice(h * n, (h + 1) * n)
        r = r_ref[0, :, sl].astype(F32)
        k = k_ref[0, :, sl].astype(F32)
        v = v_ref[0, :, sl].astype(F32)
        a = a_ref[0, :, sl].astype(F32)
        lw = lw_ref[0, :, sl]
        cl = jnp.dot(tril_ones, lw, preferred_element_type=F32, precision=HI)
        w_inc = jnp.exp(cl)
        w_inv = jnp.exp(-cl)
        w_prev = jnp.exp(cl - lw)
        kk = k * kk_ref[:, sl]
        kk = kk / jnp.maximum(jnp.sqrt(jnp.sum(kk * kk, axis=-1, keepdims=True)), 1e-12)
        k2 = k * (1.0 + (a - 1.0) * ka_ref[:, sl])
        rt = (r * w_inc).astype(BF16)
        kkt = (kk * w_prev).astype(BF16)
        bt = (kk * a * w_inv).astype(BF16)
        kt = (k2 * w_inv).astype(BF16)
        vb = v.astype(BF16)

        l_b = jnp.where(strict, _dot_nt(kkt, bt), 0.0)
        l_k = jnp.where(strict, _dot_nt(kkt, kt), 0.0)
        p_b = jnp.where(lower, _dot_nt(rt, bt), 0.0)
        p_k = jnp.where(lower, _dot_nt(rt, kt), 0.0)

        t_inv = eye - l_b
        pw = _dot(l_b, l_b)
        span = 2
        while True:
            t_inv = t_inv + _dot(t_inv, pw)
            span *= 2
            if span >= c:
                break
            pw = _dot(pw, pw)

        s0 = state[h]
        s0b = s0.astype(BF16)
        u = -_dot(t_inv, _dot_nt(kkt, s0b) + _dot(l_k, vb))
        y = _dot_nt(rt, s0b) + _dot(p_b, u) + _dot(p_k, vb)
        s_new = s0 + _dot_tn(u, bt) + _dot_tn(vb, kt)
        state[h] = s_new * w_inc[c - 1:, :]

        ym = jnp.mean(y, axis=-1, keepdims=True)
        yc = y - ym
        yv = jnp.mean(yc * yc, axis=-1, keepdims=True)
        yn = yc * lax.rsqrt(yv + RWKV_GN_EPS) * lng_ref[:, sl] + lnb_ref[:, sl]
        bonus = jnp.sum(r * k2 * rk_ref[:, sl], axis=-1, keepdims=True) * v
        y_ref[0, :, sl] = ((yn + bonus) * g_ref[0, :, sl].astype(F32)).astype(y_ref.dtype)


def _rwkv_recurrence(r, k, v, a, lw, g, k_k, k_a, r_k, ln_g, ln_b, *, heads_per_step=2):
    bsz, seq, dr = r.shape
    n = k_k.shape[-1]
    hb = heads_per_step
    wblk = hb * n
    assert wblk % LANES == 0 and dr % wblk == 0 and seq % RWKV_CHUNK == 0
    c = RWKV_CHUNK
    tok = pl.BlockSpec((1, c, wblk), lambda b, h, t: (b, t, h))
    par = pl.BlockSpec((1, wblk), lambda b, h, t: (0, h))
    flat = lambda p: p.reshape(1, dr).astype(F32)
    return pl.pallas_call(
        functools.partial(_rwkv_rec_body, heads=hb, n=n),
        grid=(bsz, dr // wblk, seq // c),
        in_specs=[tok] * 6 + [par] * 5,
        out_specs=tok,
        out_shape=jax.ShapeDtypeStruct((bsz, seq, dr), BF16),
        scratch_shapes=[pltpu.VMEM((hb, n, n), F32)],
        compiler_params=_cp(("parallel", "parallel", "arbitrary")),
    )(r, k, v, a, lw, g, flat(k_k), flat(k_a), flat(r_k), flat(ln_g), flat(ln_b))


def _s5_param_body(are_ref, aim_ref, ldt_ref, bre_ref, bim_ref, cre_ref, cim_ref, d_ref,
                   t_ref, bmat_ref, cmat_ref, q_ref, *, lc, n_levels):
    a_re, a_im = are_ref[0], aim_ref[0]
    dt = jnp.exp(ldt_ref[0])
    lam = a_re * dt
    ang = a_im * dt
    mag = jnp.exp(lam)
    ab_re, ab_im = mag * jnp.cos(ang), mag * jnp.sin(ang)
    den = a_re * a_re + a_im * a_im
    f_re = ((ab_re - 1.0) * a_re + ab_im * a_im) / den
    f_im = (ab_im * a_re - (ab_re - 1.0) * a_im) / den
    b_re, b_im = bre_ref[0], bim_ref[0]
    bb_re = f_re * b_re - f_im * b_im
    bb_im = f_re * b_im + f_im * b_re
    c_re, c_im = cre_ref[0], cim_ref[0]
    nh, p2 = b_re.shape
    lo = lax.broadcasted_iota(I32, (1, p2), 1) < (p2 // 2)

    def power(tf):
        m = jnp.exp(tf * lam)
        return m * jnp.cos(tf * ang), m * jnp.sin(tf * ang)

    def cmul3(xr, xi, pr, pi):
        rr = xr[None] * pr[:, None, :] - xi[None] * pi[:, None, :]
        ri = xr[None] * pi[:, None, :] + xi[None] * pr[:, None, :]
        return rr.reshape(lc * nh, p2), ri.reshape(lc * nh, p2)

    tcol = lax.broadcasted_iota(I32, (lc, 1), 0).astype(F32)
    nr, ni = power(-tcol)
    pr, pi = power(tcol)
    bm_re, bm_im = cmul3(bb_re, bb_im, nr, ni)
    cm_re, cm_im = cmul3(c_re, c_im, pr, pi)
    x = jnp.where(lo, bm_re, -bm_im)
    yc = jnp.where(lo, cm_re, cm_im)
    t = _dot_nt(x, yc, precision=HI)
    m = lc * nh
    ri = lax.broadcasted_iota(I32, (m, m), 0)
    ci = lax.broadcasted_iota(I32, (m, m), 1)
    t = jnp.where(ci // nh >= ri // nh, t, 0.0)
    t = jnp.where(ri == ci, t + d_ref[0], t)
    t_ref[0] = t.astype(t_ref.dtype)

    er, ei = power(jnp.full((1, 1), lc - 1.0, F32))
    bmat_ref[0] = jnp.where(lo, bm_re * er - bm_im * ei, bm_re * ei + bm_im * er).astype(bmat_ref.dtype)
    cmat_ref[0] = jnp.where(lo, cm_re * ab_re - cm_im * ab_im,
                            -(cm_re * ab_im + cm_im * ab_re)).astype(cmat_ref.dtype)
    for i in range(n_levels):
        qr, qi = power(jnp.full((1, 1), float(lc * 2 ** i), F32))
        q_ref[0, i:i + 1, :] = qr
        q_ref[0, n_levels + i:n_levels + i + 1, :] = jnp.where(lo, -qi, qi)


def _s5_conv_body(u_ref, t_ref, bmat_ref, cmat_ref, q_ref, z_ref, *, nc, n_levels):
    u = u_ref[0]
    zloc = jnp.dot(u, bmat_ref[0], preferred_element_type=F32)
    rows, p2 = zloc.shape
    n_in_seq = lax.broadcasted_iota(I32, (rows, p2), 0) % nc
    s = jnp.where(n_in_seq >= 1, pltpu.roll(zloc, 1, 0), 0.0)
    for i in range(n_levels):
        step = 2 ** i
        sh = jnp.where(n_in_seq >= step, pltpu.roll(s, step, 0), 0.0)
        sw = pltpu.roll(sh, p2 // 2, 1)
        s = s + q_ref[0, i:i + 1, :] * sh + q_ref[0, n_levels + i:n_levels + i + 1, :] * sw
    y = jnp.dot(u, t_ref[0], preferred_element_type=F32) + _dot_nt(s.astype(BF16), cmat_ref[0])
    z_ref[0] = jax.nn.gelu(y).astype(z_ref.dtype)


def _s5_branch(p_u, a_re, a_im, log_dt, b_re, b_im, c_re, c_im, d):
    bsz, seq, ds = p_u.shape
    g, p = a_re.shape
    nh = ds // g
    lc = min(S5_CHUNK, seq)
    nc = seq // lc
    n_levels = max(1, (nc - 1).bit_length())
    rows = bsz * nc
    m = lc * nh
    tile2 = lambda a: jnp.concatenate([a, a], axis=-1).astype(F32)
    args = (tile2(a_re)[:, None, :], tile2(a_im)[:, None, :], log_dt.reshape(g, 1, 1).astype(F32),
            tile2(jnp.swapaxes(b_re, 1, 2)), tile2(jnp.swapaxes(b_im, 1, 2)),
            tile2(c_re), tile2(c_im), jnp.tile(d.astype(F32), (1, lc))[:, None, :])
    gspec = lambda a: pl.BlockSpec((1,) + a.shape[1:], lambda i: (i, 0, 0))
    qrows = 2 * n_levels
    t_mat, bmat, cmat, q = pl.pallas_call(
        functools.partial(_s5_param_body, lc=lc, n_levels=n_levels),
        grid=(g,),
        in_specs=[gspec(a) for a in args],
        out_specs=[pl.BlockSpec((1, m, m), lambda i: (i, 0, 0)),
                   pl.BlockSpec((1, m, 2 * p), lambda i: (i, 0, 0)),
                   pl.BlockSpec((1, m, 2 * p), lambda i: (i, 0, 0)),
                   pl.BlockSpec((1, qrows, 2 * p), lambda i: (i, 0, 0))],
        out_shape=[jax.ShapeDtypeStruct((g, m, m), BF16),
                   jax.ShapeDtypeStruct((g, m, 2 * p), BF16),
                   jax.ShapeDtypeStruct((g, m, 2 * p), BF16),
                   jax.ShapeDtypeStruct((g, qrows, 2 * p), F32)],
        compiler_params=_cp(("parallel",)),
    )(*args)

    u = p_u.reshape(bsz, nc, lc, g, nh).transpose(3, 0, 1, 2, 4).reshape(g, rows, m)
    z = pl.pallas_call(
        functools.partial(_s5_conv_body, nc=nc, n_levels=n_levels),
        grid=(g,),
        in_specs=[gspec(u), gspec(t_mat), gspec(bmat), gspec(cmat), gspec(q)],
        out_specs=pl.BlockSpec((1, rows, m), lambda i: (i, 0, 0)),
        out_shape=jax.ShapeDtypeStruct((g, rows, m), BF16),
        compiler_params=_cp(("parallel",)),
    )(u, t_mat, bmat, cmat, q)
    return z.reshape(g, bsz, nc, lc, nh).transpose(1, 2, 3, 0, 4).reshape(bsz, seq, ds)


def _merge_body(yr_ref, z_ref, wp_ref, w1_ref, w2_ref, gr_ref, gs_ref, o_ref):
    y_rwkv = jnp.dot(yr_ref[...], wp_ref[...], preferred_element_type=F32)
    z = z_ref[...]
    y_s5 = (jnp.dot(z, w1_ref[...], preferred_element_type=F32)
            * _sigmoid(jnp.dot(z, w2_ref[...], preferred_element_type=F32)))
    o_ref[...] = (_sigmoid(gr_ref[...].astype(F32)) * y_rwkv
                  + _sigmoid(gs_ref[...].astype(F32)) * y_s5).astype(o_ref.dtype)


def _merge(yr, z, wp, w1, w2, p_gates, *, tm=512, tn=512):
    t, kr = yr.shape
    ks = z.shape[1]
    d = wp.shape[1]
    tm, tn = min(tm, t), min(tn, d)
    nj = d // tn
    return pl.pallas_call(
        _merge_body,
        grid=(t // tm, nj),
        in_specs=[pl.BlockSpec((tm, kr), lambda i, j: (i, 0)),
                  pl.BlockSpec((tm, ks), lambda i, j: (i, 0)),
                  pl.BlockSpec((kr, tn), lambda i, j: (0, j)),
                  pl.BlockSpec((ks, tn), lambda i, j: (0, j)),
                  pl.BlockSpec((ks, tn), lambda i, j: (0, j)),
                  pl.BlockSpec((tm, tn), lambda i, j: (i, j)),
                  pl.BlockSpec((tm, tn), lambda i, j: (i, j + nj))],
        out_specs=pl.BlockSpec((tm, tn), lambda i, j: (i, j)),
        out_shape=jax.ShapeDtypeStruct((t, d), BF16),
        compiler_params=_cp(("parallel", "parallel")),
    )(yr, z, wp, w1, w2, p_gates, p_gates)


def _router_body(x_ref, mix_ref, g1_ref, lng_ref, lnb_ref, sc_ref, sh_ref, rw_ref, rb_ref,
                 x1_ref, h2_ref, idx_ref, w_ref, *, alpha):
    x1 = _ln(alpha * x_ref[0] + g1_ref[0] * mix_ref[0].astype(F32)) * lng_ref[...] + lnb_ref[...]
    x1_ref[0] = x1
    h2 = _ln(x1) * (1.0 + sc_ref[0]) + sh_ref[0]
    h2_ref[0] = h2.astype(h2_ref.dtype)
    logits = jnp.dot(h2, rw_ref[...], preferred_element_type=F32, precision=HI) + rb_ref[...]
    lane = lax.broadcasted_iota(I32, logits.shape, 1)
    lane_f = lane.astype(F32)
    idx_out = jnp.zeros(logits.shape, I32)
    val_out = jnp.zeros(logits.shape, F32)
    vals = []
    for kth in range(TOP_K):
        mx = jnp.max(logits, axis=-1, keepdims=True)
        sel = jnp.min(jnp.where(logits == mx, lane_f, float(LANES)), axis=-1, keepdims=True).astype(I32)
        vals.append(mx)
        idx_out = jnp.where(lane == kth, sel, idx_out)
        logits = jnp.where(lane == sel, -jnp.inf, logits)
    exps = [jnp.exp(vv - vals[0]) for vv in vals]
    tot = exps[0]
    for e in exps[1:]:
        tot = tot + e
    for kth in range(TOP_K):
        val_out = jnp.where(lane == kth, exps[kth] / tot, val_out)
    idx_ref[0] = idx_out
    w_ref[0] = val_out


def _resid_ln_router(x, mix, mod3, ln_g, ln_b, router_w, router_b, *, alpha, tl=256):
    bsz, seq, d = x.shape
    e = router_w.shape[1]
    tl = min(tl, seq)
    rw = jnp.zeros((d, LANES), F32).at[:, :e].set(router_w.astype(F32))
    rb = jnp.full((1, LANES), -jnp.inf, F32).at[0, :e].set(router_b.astype(F32))
    tok = lambda w: pl.BlockSpec((1, tl, w), lambda b, l: (b, l, 0))
    modspec = lambda j: pl.BlockSpec((1, 1, d), lambda b, l: (b * 6 + j, 0, 0))
    full = lambda a: pl.BlockSpec(a.shape, lambda b, l: (0,) * a.ndim)
    args = (x, mix, mod3, ln_g[None].astype(F32), ln_b[None].astype(F32), mod3, mod3, rw, rb)
    return pl.pallas_call(
        functools.partial(_router_body, alpha=alpha),
        grid=(bsz, seq // tl),
        in_specs=[tok(d), tok(d), modspec(2), full(args[3]), full(args[4]), modspec(4), modspec(3),
                  full(rw), full(rb)],
        out_specs=[tok(d), tok(d), tok(LANES), tok(LANES)],
        out_shape=[jax.ShapeDtypeStruct((bsz, seq, d), F32),
                   jax.ShapeDtypeStruct((bsz, seq, d), BF16),
                   jax.ShapeDtypeStruct((bsz, seq, LANES), I32),
                   jax.ShapeDtypeStruct((bsz, seq, LANES), F32)],
        compiler_params=_cp(("parallel", "parallel")),
    )(*args)


def _gather_body(idx_ref, src_ref, o_ref, sem, *, rows_per_step):
    base = pl.program_id(0) * rows_per_step

    def row_copy(r):
        return pltpu.make_async_copy(src_ref.at[idx_ref[base + r]], o_ref.at[base + r], sem)

    def start(r, carry):
        row_copy(r).start()
        return carry

    def wait(r, carry):
        row_copy(r).wait()
        return carry

    lax.fori_loop(0, rows_per_step, start, 0)
    lax.fori_loop(0, rows_per_step, wait, 0)


def _gather_rows(src, idx, *, rows_per_step=256):
    n = idx.shape[0]
    width = src.shape[1]
    rows_per_step = min(rows_per_step, n)
    assert n % rows_per_step == 0 and width % LANES == 0
    src = src.reshape(src.shape[0], width // LANES, LANES)
    return pl.pallas_call(
        functools.partial(_gather_body, rows_per_step=rows_per_step),
        grid_spec=pltpu.PrefetchScalarGridSpec(
            num_scalar_prefetch=1,
            grid=(n // rows_per_step,),
            in_specs=[pl.BlockSpec(memory_space=pl.ANY)],
            out_specs=pl.BlockSpec(memory_space=pl.ANY),
            scratch_shapes=[pltpu.SemaphoreType.DMA(())],
        ),
        out_shape=jax.ShapeDtypeStruct((n,) + src.shape[1:], src.dtype),
        compiler_params=_cp(("arbitrary",)),
    )(idx, src).reshape(n, width)


def _expert_up_body(be_ref, x_ref, wg_ref, bg_ref, wu_ref, bu_ref, o_ref):
    x = x_ref[...]
    gate = jnp.dot(x, wg_ref[0], preferred_element_type=F32) + bg_ref[0]
    up = jnp.dot(x, wu_ref[0], preferred_element_type=F32) + bu_ref[0]
    gate = jnp.minimum(gate, SWIGLU_LIMIT)
    up = jnp.clip(up, -SWIGLU_LIMIT, SWIGLU_LIMIT)
    o_ref[...] = ((up + 1.0) * gate * _sigmoid(SWIGLU_ALPHA * gate)).astype(o_ref.dtype)


def _expert_down_body(be_ref, a_ref, wd_ref, bd_ref, rw_ref, o_ref):
    y = jnp.dot(a_ref[...], wd_ref[0], preferred_element_type=F32) + bd_ref[0]
    o_ref[...] = (y * rw_ref[...]).astype(o_ref.dtype)


def _experts(xs, block_e, row_w, w_gate, b_gate, w_up, b_up, w_down, b_down):
    n_rows, d = xs.shape
    e, _, f = w_gate.shape
    nb = n_rows // EXPERT_BLOCK
    rows = lambda w: pl.BlockSpec((EXPERT_BLOCK, w), lambda i, be: (i, 0))
    byexp = lambda s: pl.BlockSpec((1,) + s, lambda i, be: (be[i], 0, 0))
    act = pl.pallas_call(
        _expert_up_body,
        grid_spec=pltpu.PrefetchScalarGridSpec(
            num_scalar_prefetch=1, grid=(nb,),
            in_specs=[rows(d), byexp((d, f)), byexp((1, f)), byexp((d, f)), byexp((1, f))],
            out_specs=rows(f)),
        out_shape=jax.ShapeDtypeStruct((n_rows, f), BF16),
        compiler_params=_cp(("arbitrary",)),
    )(block_e, xs, w_gate.astype(BF16), b_gate[:, None, :].astype(F32),
      w_up.astype(BF16), b_up[:, None, :].astype(F32))
    return pl.pallas_call(
        _expert_down_body,
        grid_spec=pltpu.PrefetchScalarGridSpec(
            num_scalar_prefetch=1, grid=(nb,),
            in_specs=[rows(f), byexp((f, d)), byexp((1, d)), rows(1)],
            out_specs=rows(d)),
        out_shape=jax.ShapeDtypeStruct((n_rows, d), BF16),
        compiler_params=_cp(("arbitrary",)),
    )(block_e, act, w_down.astype(BF16), b_down[:, None, :].astype(F32), row_w[:, None])


def _final_body(x1_ref, y_ref, g2_ref, lng_ref, lnb_ref, o_ref, *, alpha):
    ff = y_ref[0].astype(F32)
    for kth in range(1, TOP_K):
        ff = ff + y_ref[kth].astype(F32)
    o_ref[0] = (_ln(alpha * x1_ref[0] + g2_ref[0] * ff) * lng_ref[...] + lnb_ref[...]).astype(o_ref.dtype)


def _final(x1, ysel, mod3, ln_g, ln_b, out_dtype, *, alpha, tl=256):
    bsz, seq, d = x1.shape
    tl = min(tl, seq)
    nl = seq // tl
    full = lambda a: pl.BlockSpec(a.shape, lambda b, l: (0,) * a.ndim)
    lg, lb = ln_g[None].astype(F32), ln_b[None].astype(F32)
    return pl.pallas_call(
        functools.partial(_final_body, alpha=alpha),
        grid=(bsz, nl),
        in_specs=[pl.BlockSpec((1, tl, d), lambda b, l: (b, l, 0)),
                  pl.BlockSpec((TOP_K, tl, d), lambda b, l: (0, b * nl + l, 0)),
                  pl.BlockSpec((1, 1, d), lambda b, l: (b * 6 + 5, 0, 0)),
                  full(lg), full(lb)],
        out_specs=pl.BlockSpec((1, tl, d), lambda b, l: (b, l, 0)),
        out_shape=jax.ShapeDtypeStruct((bsz, seq, d), out_dtype),
        compiler_params=_cp(("parallel", "parallel")),
    )(x1, ysel, mod3, lg, lb)


def _route(top_idx, top_w, n_experts):
    n_tok = top_idx.shape[0]
    n_assign = n_tok * TOP_K
    flat_e = top_idx.reshape(n_assign)
    flat_tok = jnp.arange(n_assign, dtype=I32) // TOP_K
    flat_w = top_w.reshape(n_assign)
    order = jnp.argsort(flat_e)
    sorted_e = flat_e[order]
    counts = jnp.bincount(flat_e, length=n_experts)
    padded = (counts + EXPERT_BLOCK - 1) // EXPERT_BLOCK * EXPERT_BLOCK
    start = jnp.cumsum(counts) - counts
    pad_end = jnp.cumsum(padded)
    pad_start = pad_end - padded
    dest = (pad_start[sorted_e] + jnp.arange(n_assign, dtype=I32) - start[sorted_e]).astype(I32)
    n_blocks = -(-(n_assign + n_experts * (EXPERT_BLOCK - 1)) // EXPERT_BLOCK)
    n_rows = n_blocks * EXPERT_BLOCK
    row_tok = jnp.zeros((n_rows,), I32).at[dest].set(flat_tok[order])
    row_w = jnp.zeros((n_rows,), F32).at[dest].set(flat_w[order])
    block_start = jnp.arange(n_blocks, dtype=pad_end.dtype) * EXPERT_BLOCK
    block_e = jnp.minimum(jnp.searchsorted(pad_end, block_start, side='right'),
                          n_experts - 1).astype(I32)
    pos = jnp.zeros((n_assign,), I32).at[order].set(dest)
    pos_kmajor = pos.reshape(n_tok, TOP_K).T.reshape(n_assign)
    return row_tok, row_w, block_e, pos_kmajor


def kernel(x, c, ada_w, ada_b, w_in, shift_mu, rwkv_w0, rwkv_w2, rwkv_a0, rwkv_a2, rwkv_g2,
           rwkv_k_k, rwkv_k_a, rwkv_r_k, rwkv_ln_g, rwkv_ln_b, rwkv_proj,
           s5_a_re, s5_a_im, s5_log_dt, s5_b_re, s5_b_im, s5_c_re, s5_c_im, s5_d,
           s5_glu_w1, s5_glu_w2, w_out, ln1_g, ln1_b, ln2_g, ln2_b, router_w, router_b,
           exp_w_gate, exp_b_gate, exp_w_up, exp_b_up, exp_w_down, exp_b_down):
    out_dtype = x.dtype
    bsz, seq, d = x.shape
    depth = ada_w.shape[0]
    alpha = (2 * depth) ** 0.25
    dr = rwkv_w0.shape[1]
    ds = s5_glu_w1.shape[1]
    shift_w = shift_mu.shape[1]
    n_tok = bsz * seq
    n_experts = router_w.shape[2]
    x = x.astype(F32)
    c_pad = jnp.zeros((8, d), F32).at[:bsz].set(c.astype(F32))

    for l in range(depth):
        mod = _matmul(c_pad, ada_w[l], ada_b[l][None].astype(F32), out_dtype=F32, tm=8, tn=512,
                      silu_a=True)[:bsz]
        mod3 = mod.reshape(bsz * 6, 1, d)

        h = _ln_mod(x, mod3, 1, 0).reshape(n_tok, d)
        wi = w_in[l].astype(BF16)
        proj = lambda lo, hi, tn: _matmul(h, wi[:, lo:hi], out_dtype=BF16, tm=1024, tn=tn)
        p_rkv = proj(0, 3 * dr, 512).reshape(bsz, seq, 3 * dr)
        p_lora = proj(3 * dr, shift_w, 256).reshape(bsz, seq, shift_w - 3 * dr)
        p_u = proj(shift_w, shift_w + ds, 512).reshape(bsz, seq, ds)
        p_gates = proj(shift_w + ds, shift_w + ds + 2 * d, 512)

        r, k, v, a, lw, g = _rwkv_prep(p_rkv, p_lora, shift_mu[l].astype(F32), rwkv_w0[l].astype(F32),
                                       rwkv_w2[l], rwkv_a0[l].astype(F32), rwkv_a2[l], rwkv_g2[l])
        yr = _rwkv_recurrence(r, k, v, a, lw, g, rwkv_k_k[l], rwkv_k_a[l], rwkv_r_k[l],
                              rwkv_ln_g[l], rwkv_ln_b[l])
        z = _s5_branch(p_u, s5_a_re[l], s5_a_im[l], s5_log_dt[l], s5_b_re[l], s5_b_im[l],
                       s5_c_re[l], s5_c_im[l], s5_d[l])
        merged = _merge(yr.reshape(n_tok, dr), z.reshape(n_tok, ds), rwkv_proj[l].astype(BF16),
                        s5_glu_w1[l].astype(BF16), s5_glu_w2[l].astype(BF16), p_gates)
        mix = _matmul(merged, w_out[l].astype(BF16), out_dtype=F32, tm=1024, tn=512)

        x1, h2, top_idx, top_w = _resid_ln_router(x, mix.reshape(bsz, seq, d), mod3, ln1_g[l], ln1_b[l],
                                                  router_w[l], router_b[l], alpha=alpha)
        row_tok, row_w, block_e, pos_kmajor = _route(
            top_idx.reshape(n_tok, LANES)[:, :TOP_K], top_w.reshape(n_tok, LANES)[:, :TOP_K], n_experts)
        xs = _gather_rows(h2.reshape(n_tok, d), row_tok)
        ys = _experts(xs, block_e, row_w, exp_w_gate[l], exp_b_gate[l], exp_w_up[l], exp_b_up[l],
                      exp_w_down[l], exp_b_down[l])
        ysel = _gather_rows(ys, pos_kmajor).reshape(TOP_K, n_tok, d)
        x = _final(x1, ysel, mod3, ln2_g[l], ln2_b[l], F32, alpha=alpha)
    return x.astype(out_dtype)
```

```python
import functools

import jax
import jax.numpy as jnp
from jax import lax
from jax.experimental import pallas as pl
from jax.experimental.pallas import tpu as pltpu

F32 = jnp.float32
BF16 = jnp.bfloat16
I32 = jnp.int32

LN_EPS = 1e-5
RWKV_GN_EPS = 64e-5
TOP_K = 4
EXPERT_BLOCK = 256
SWIGLU_LIMIT = 7.0
SWIGLU_ALPHA = 1.702
RWKV_CHUNK = 64
S5_CHUNK = 64
LANES = 128
VMEM_LIMIT = 56 * 1024 * 1024
HI = lax.Precision.HIGHEST


def _cp(sem, vmem=VMEM_LIMIT):
    return pltpu.CompilerParams(dimension_semantics=sem, vmem_limit_bytes=vmem)


def _dot(a, b):
    return jnp.dot(a.astype(BF16), b.astype(BF16), preferred_element_type=F32)


def _dot_nt(a, b, precision=None):
    return lax.dot_general(a, b, (((1,), (1,)), ((), ())), preferred_element_type=F32,
                           precision=precision)


def _dot_tn(a, b):
    return lax.dot_general(a.astype(BF16), b.astype(BF16), (((0,), (0,)), ((), ())),
                           preferred_element_type=F32)


def _ln(x):
    mean = jnp.mean(x, axis=-1, keepdims=True)
    xc = x - mean
    var = jnp.mean(xc * xc, axis=-1, keepdims=True)
    return xc * lax.rsqrt(var + LN_EPS)


def _sigmoid(x):
    return 1.0 / (1.0 + jnp.exp(-x))


def _tile(n, pref, align):
    if n <= pref:
        return n
    t = pref - pref % align
    while n % t:
        t -= align
    return t


def _mm_body(a_ref, b_ref, bias_ref, o_ref, *, silu_a):
    a = a_ref[...]
    if silu_a:
        a = a * _sigmoid(a)
    o_ref[...] = (_dot(a, b_ref[...]) + bias_ref[...]).astype(o_ref.dtype)


def _matmul(a, b, bias=None, *, out_dtype, tm, tn, name, silu_a=False):
    m, k = a.shape
    n = b.shape[1]
    tm, tn = _tile(m, tm, 8), _tile(n, tn, LANES)
    if bias is None:
        bias = jnp.zeros((1, n), F32)
    return pl.pallas_call(
        functools.partial(_mm_body, silu_a=silu_a),
        grid=(m // tm, n // tn),
        in_specs=[pl.BlockSpec((tm, k), lambda i, j: (i, 0)),
                  pl.BlockSpec((k, tn), lambda i, j: (0, j)),
                  pl.BlockSpec((1, tn), lambda i, j: (0, j))],
        out_specs=pl.BlockSpec((tm, tn), lambda i, j: (i, j)),
        out_shape=jax.ShapeDtypeStruct((m, n), out_dtype),
        compiler_params=_cp(("parallel", "parallel")),
        name=name,
    )(a, b, bias)


def _ln_mod_body(x_ref, sc_ref, sh_ref, o_ref):
    y = _ln(x_ref[0])
    o_ref[0] = (y * (1.0 + sc_ref[0]) + sh_ref[0]).astype(o_ref.dtype)


def _ln_mod(x, mod3, sc_idx, sh_idx, *, tl=256):
    bsz, seq, d = x.shape
    tl = min(tl, seq)
    return pl.pallas_call(
        _ln_mod_body,
        grid=(bsz, seq // tl),
        in_specs=[pl.BlockSpec((1, tl, d), lambda b, l: (b, l, 0)),
                  pl.BlockSpec((1, 1, d), lambda b, l: (b * 6 + sc_idx, 0, 0)),
                  pl.BlockSpec((1, 1, d), lambda b, l: (b * 6 + sh_idx, 0, 0))],
        out_specs=pl.BlockSpec((1, tl, d), lambda b, l: (b, l, 0)),
        out_shape=jax.ShapeDtypeStruct((bsz, seq, d), BF16),
        compiler_params=_cp(("parallel", "parallel")),
        name="ln_modulate",
    )(x, mod3, mod3)


def _token_shift(x, carry_ref, mu):
    rolled = pltpu.roll(x, 1, 0)
    row = lax.broadcasted_iota(I32, x.shape, 0)
    prev = jnp.where(row == 0, carry_ref[...], rolled)
    carry_ref[...] = x[x.shape[0] - 1:, :]
    return x + (prev - x) * mu


def _rwkv_prep_body(prkv_ref, plora_ref, mu_rkv_ref, mu_lora_ref, w0_ref, w2_ref, a0_ref, a2_ref,
                    g2_ref, r_ref, k_ref, v_ref, a_ref, lw_ref, g_ref, carry_rkv, carry_lora,
                    *, dr, w_lora, a_lora):
    @pl.when(pl.program_id(1) == 0)
    def _():
        carry_rkv[...] = jnp.zeros_like(carry_rkv)
        carry_lora[...] = jnp.zeros_like(carry_lora)

    xs = _token_shift(prkv_ref[0].astype(F32), carry_rkv, mu_rkv_ref[...])
    r_ref[0] = xs[:, :dr].astype(r_ref.dtype)
    k_ref[0] = xs[:, dr:2 * dr].astype(k_ref.dtype)
    v_ref[0] = xs[:, 2 * dr:].astype(v_ref.dtype)

    ls = _token_shift(plora_ref[0].astype(F32), carry_lora, mu_lora_ref[...])
    xw = ls[:, :w_lora]
    xa = ls[:, w_lora:w_lora + a_lora]
    xg = ls[:, w_lora + a_lora:]
    z = -(w0_ref[...] + _dot(jnp.tanh(xw), w2_ref[...]))
    softplus = jnp.maximum(z, 0.0) + jnp.log1p(jnp.exp(-jnp.abs(z)))
    lw_ref[0] = -jnp.exp(-softplus - 0.5)
    a_ref[0] = _sigmoid(a0_ref[...] + _dot(xa, a2_ref[...])).astype(a_ref.dtype)
    g_ref[0] = _dot(_sigmoid(xg), g2_ref[...]).astype(g_ref.dtype)


def _rwkv_prep(p_rkv, p_lora, mu, w0, w2, a0, a2, g2, *, tl=256):
    bsz, seq, dr3 = p_rkv.shape
    dr = dr3 // 3
    lora = p_lora.shape[-1]
    w_lora, a_lora = w2.shape[0], a2.shape[0]
    tl = min(tl, seq)
    tok = lambda w: pl.BlockSpec((1, tl, w), lambda b, l: (b, l, 0))
    full = lambda a: pl.BlockSpec(a.shape, lambda b, l: (0,) * a.ndim)
    mu_rkv, mu_lora = mu[None, :dr3], mu[None, dr3:]
    args = (p_rkv, p_lora, mu_rkv, mu_lora, w0[None], w2.astype(BF16), a0[None], a2.astype(BF16),
            g2.astype(BF16))
    out = jax.ShapeDtypeStruct((bsz, seq, dr), BF16)
    return pl.pallas_call(
        functools.partial(_rwkv_prep_body, dr=dr, w_lora=w_lora, a_lora=a_lora),
        grid=(bsz, seq // tl),
        in_specs=[tok(dr3), tok(lora)] + [full(a) for a in args[2:]],
        out_specs=[tok(dr)] * 6,
        out_shape=[out, out, out, out, jax.ShapeDtypeStruct((bsz, seq, dr), F32), out],
        scratch_shapes=[pltpu.VMEM((1, dr3), F32), pltpu.VMEM((1, lora), F32)],
        compiler_params=_cp(("parallel", "arbitrary")),
        name="rwkv_prep",
    )(*args)


def _split_bf16(x):
    hi = x.astype(BF16)
    return hi, (x - hi.astype(F32)).astype(BF16)


def _head_sum(x, bd_ref):
    c, w = x.shape
    gw = bd_ref.shape[0]
    ng = w // gw
    hi, lo = _split_bf16(x)
    groups = [p[:, j * gw:(j + 1) * gw] for p in (hi, lo) for j in range(ng)]
    res = jnp.dot(jnp.concatenate(groups, axis=0), bd_ref[...], preferred_element_type=F32)
    parts = [res[j * c:(j + 1) * c] + res[(ng + j) * c:(ng + j + 1) * c] for j in range(ng)]
    return jnp.concatenate(parts, axis=-1)


def _rwkv_rec_body(r_ref, k_ref, v_ref, a_ref, lw_ref, g_ref, kk_ref, ka_ref, rk_ref, lng_ref,
                   lnb_ref, bd_ref, y_ref, state, ybuf, *, heads, n):
    c = r_ref.shape[1]

    @pl.when(pl.program_id(2) == 0)
    def _():
        state[...] = jnp.zeros_like(state)

    row = lax.broadcasted_iota(I32, (c, c), 0)
    col = lax.broadcasted_iota(I32, (c, c), 1)
    strict = row > col
    eye = (row == col).astype(F32)
    tril_b = (row >= col).astype(BF16)
    row2 = lax.broadcasted_iota(I32, (c, 2 * c), 0)
    col2 = lax.broadcasted_iota(I32, (c, 2 * c), 1)
    strict_right = (col2 >= c) & (row2 > col2 - c)
    lower_pair = row2 >= jnp.where(col2 >= c, col2 - c, col2)

    r = r_ref[0].astype(F32)
    k = k_ref[0].astype(F32)
    v = v_ref[0].astype(F32)
    a = a_ref[0].astype(F32)
    lw = lw_ref[0]
    lw_hi, lw_lo = _split_bf16(lw)
    cl = (jnp.dot(tril_b, lw_hi, preferred_element_type=F32)
          + jnp.dot(tril_b, lw_lo, preferred_element_type=F32))
    w_inc = jnp.exp(cl)
    w_inv = jnp.exp(-cl)
    w_prev = jnp.exp(cl - lw)
    kk = k * kk_ref[...]
    kk = kk / jnp.maximum(jnp.sqrt(_head_sum(kk * kk, bd_ref)), 1e-12)
    k2 = k * (1.0 + (a - 1.0) * ka_ref[...])
    rt = (r * w_inc).astype(BF16)
    kkt = (kk * w_prev).astype(BF16)
    bt = (kk * a * w_inv).astype(BF16)
    kt = (k2 * w_inv).astype(BF16)
    vb = v.astype(BF16)
    bonus = _head_sum(r * k2 * rk_ref[...], bd_ref) * v

    for h in range(heads):
        sl = slice(h * n, (h + 1) * n)
        lhs2 = jnp.concatenate([kkt[:, sl], rt[:, sl]], axis=0)
        rhs2 = jnp.concatenate([bt[:, sl], kt[:, sl]], axis=0)
        vh = vb[:, sl]
        gram = _dot_nt(lhs2, rhs2)
        l_b = jnp.where(strict, gram[:c, :c], 0.0)
        lk_only = jnp.where(strict_right, gram[:c], 0.0).astype(BF16)
        p_bk = jnp.where(lower_pair, gram[c:], 0.0).astype(BF16)

        t_inv = eye - l_b
        pw = _dot(l_b, l_b)
        span = 4
        while span < c:
            both = _dot(jnp.concatenate([t_inv, pw], axis=0), pw)
            t_inv = t_inv + both[:c]
            pw = both[c:]
            span *= 2
        t_inv = t_inv + _dot(t_inv, pw)

        s0 = state[h]
        m1 = _dot_nt(lhs2, s0.astype(BF16))
        u = -_dot(t_inv, m1[:c] + jnp.dot(lk_only, jnp.concatenate([vh, vh], axis=0),
                                          preferred_element_type=F32))
        uv = jnp.concatenate([u.astype(BF16), vh], axis=0)
        ybuf[:, sl] = m1[c:] + jnp.dot(p_bk, uv, preferred_element_type=F32)
        s_new = s0 + _dot_tn(uv, rhs2)
        state[h] = s_new * w_inc[c - 1:, sl]

    y = ybuf[...]
    inv_n = 1.0 / n
    yc = y - _head_sum(y, bd_ref) * inv_n
    yv = _head_sum(yc * yc, bd_ref) * inv_n
    yn = yc * lax.rsqrt(yv + RWKV_GN_EPS) * lng_ref[...] + lnb_ref[...]
    y_ref[0] = ((yn + bonus) * g_ref[0].astype(F32)).astype(y_ref.dtype)


def _rwkv_recurrence(r, k, v, a, lw, g, k_k, k_a, r_k, ln_g, ln_b, *, heads_per_step=16):
    bsz, seq, dr = r.shape
    n = k_k.shape[-1]
    hb = min(heads_per_step, dr // n)
    wblk = hb * n
    gw = min(2 * LANES, wblk)
    assert wblk % LANES == 0 and dr % wblk == 0 and seq % RWKV_CHUNK == 0 and wblk % gw == 0
    c = RWKV_CHUNK
    tok = pl.BlockSpec((1, c, wblk), lambda b, h, t: (b, t, h))
    par = pl.BlockSpec((1, wblk), lambda b, h, t: (0, h))
    flat = lambda p: p.reshape(1, dr).astype(F32)
    gi = lax.broadcasted_iota(I32, (gw, gw), 0) // n
    gj = lax.broadcasted_iota(I32, (gw, gw), 1) // n
    block_diag = (gi == gj).astype(BF16)
    return pl.pallas_call(
        functools.partial(_rwkv_rec_body, heads=hb, n=n),
        grid=(bsz, dr // wblk, seq // c),
        in_specs=[tok] * 6 + [par] * 5 + [pl.BlockSpec((gw, gw), lambda b, h, t: (0, 0))],
        out_specs=tok,
        out_shape=jax.ShapeDtypeStruct((bsz, seq, dr), BF16),
        scratch_shapes=[pltpu.VMEM((hb, n, n), F32), pltpu.VMEM((c, wblk), F32)],
        compiler_params=_cp(("parallel", "parallel", "arbitrary")),
        name="rwkv_recurrence",
    )(r, k, v, a, lw, g, flat(k_k), flat(k_a), flat(r_k), flat(ln_g), flat(ln_b), block_diag)


def _s5_param_body(are_ref, aim_ref, ldt_ref, bre_ref, bim_ref, cre_ref, cim_ref, d_ref,
                   t_ref, bmat_ref, cmat_ref, q_ref, *, lc, n_levels):
    a_re, a_im = are_ref[0], aim_ref[0]
    dt = jnp.exp(ldt_ref[0])
    lam = a_re * dt
    ang = a_im * dt
    mag = jnp.exp(lam)
    ab_re, ab_im = mag * jnp.cos(ang), mag * jnp.sin(ang)
    den = a_re * a_re + a_im * a_im
    f_re = ((ab_re - 1.0) * a_re + ab_im * a_im) / den
    f_im = (ab_im * a_re - (ab_re - 1.0) * a_im) / den
    b_re, b_im = bre_ref[0], bim_ref[0]
    bb_re = f_re * b_re - f_im * b_im
    bb_im = f_re * b_im + f_im * b_re
    c_re, c_im = cre_ref[0], cim_ref[0]
    nh, p2 = b_re.shape
    lo = lax.broadcasted_iota(I32, (1, p2), 1) < (p2 // 2)

    def power(tf):
        m = jnp.exp(tf * lam)
        return m * jnp.cos(tf * ang), m * jnp.sin(tf * ang)

    def cmul3(xr, xi, pr, pi):
        rr = xr[None] * pr[:, None, :] - xi[None] * pi[:, None, :]
        ri = xr[None] * pi[:, None, :] + xi[None] * pr[:, None, :]
        return rr.reshape(lc * nh, p2), ri.reshape(lc * nh, p2)

    tcol = lax.broadcasted_iota(I32, (lc, 1), 0).astype(F32)
    nr, ni = power(-tcol)
    pr, pi = power(tcol)
    bm_re, bm_im = cmul3(bb_re, bb_im, nr, ni)
    cm_re, cm_im = cmul3(c_re, c_im, pr, pi)
    x = jnp.where(lo, bm_re, -bm_im)
    yc = jnp.where(lo, cm_re, cm_im)
    t = _dot_nt(x, yc, precision=HI)
    m = lc * nh
    ri = lax.broadcasted_iota(I32, (m, m), 0)
    ci = lax.broadcasted_iota(I32, (m, m), 1)
    t = jnp.where(ci // nh >= ri // nh, t, 0.0)
    t = jnp.where(ri == ci, t + d_ref[0], t)
    t_ref[0] = t.astype(t_ref.dtype)

    er, ei = power(jnp.full((1, 1), lc - 1.0, F32))
    bmat_ref[0] = jnp.where(lo, bm_re * er - bm_im * ei, bm_re * ei + bm_im * er).astype(bmat_ref.dtype)
    cmat_ref[0] = jnp.where(lo, cm_re * ab_re - cm_im * ab_im,
                            -(cm_re * ab_im + cm_im * ab_re)).astype(cmat_ref.dtype)
    for i in range(n_levels):
        qr, qi = power(jnp.full((1, 1), float(lc * 2 ** i), F32))
        q_ref[0, i:i + 1, :] = qr
        q_ref[0, n_levels + i:n_levels + i + 1, :] = jnp.where(lo, -qi, qi)


def _s5_conv_body(u_ref, t_ref, bmat_ref, cmat_ref, q_ref, z_ref, *, nc, n_levels):
    u = u_ref[0]
    zloc = jnp.dot(u, bmat_ref[0], preferred_element_type=F32)
    rows, p2 = zloc.shape
    n_in_seq = lax.broadcasted_iota(I32, (rows, p2), 0) % nc
    s = jnp.where(n_in_seq >= 1, pltpu.roll(zloc, 1, 0), 0.0)
    for i in range(n_levels):
        step = 2 ** i
        sh = jnp.where(n_in_seq >= step, pltpu.roll(s, step, 0), 0.0)
        sw = pltpu.roll(sh, p2 // 2, 1)
        s = s + q_ref[0, i:i + 1, :] * sh + q_ref[0, n_levels + i:n_levels + i + 1, :] * sw
    y = jnp.dot(u, t_ref[0], preferred_element_type=F32) + _dot_nt(s.astype(BF16), cmat_ref[0])
    z_ref[0] = jax.nn.gelu(y).astype(z_ref.dtype)


def _s5_branch(p_u, a_re, a_im, log_dt, b_re, b_im, c_re, c_im, d):
    bsz, seq, ds = p_u.shape
    g, p = a_re.shape
    nh = ds // g
    lc = min(S5_CHUNK, seq)
    nc = seq // lc
    n_levels = max(1, (nc - 1).bit_length())
    rows = bsz * nc
    m = lc * nh
    tile2 = lambda a: jnp.concatenate([a, a], axis=-1).astype(F32)
    args = (tile2(a_re)[:, None, :], tile2(a_im)[:, None, :], log_dt.reshape(g, 1, 1).astype(F32),
            tile2(jnp.swapaxes(b_re, 1, 2)), tile2(jnp.swapaxes(b_im, 1, 2)),
            tile2(c_re), tile2(c_im), jnp.tile(d.astype(F32), (1, lc))[:, None, :])
    gspec = lambda a: pl.BlockSpec((1,) + a.shape[1:], lambda i: (i, 0, 0))
    qrows = 2 * n_levels
    t_mat, bmat, cmat, q = pl.pallas_call(
        functools.partial(_s5_param_body, lc=lc, n_levels=n_levels),
        grid=(g,),
        in_specs=[gspec(a) for a in args],
        out_specs=[pl.BlockSpec((1, m, m), lambda i: (i, 0, 0)),
                   pl.BlockSpec((1, m, 2 * p), lambda i: (i, 0, 0)),
                   pl.BlockSpec((1, m, 2 * p), lambda i: (i, 0, 0)),
                   pl.BlockSpec((1, qrows, 2 * p), lambda i: (i, 0, 0))],
        out_shape=[jax.ShapeDtypeStruct((g, m, m), BF16),
                   jax.ShapeDtypeStruct((g, m, 2 * p), BF16),
                   jax.ShapeDtypeStruct((g, m, 2 * p), BF16),
                   jax.ShapeDtypeStruct((g, qrows, 2 * p), F32)],
        compiler_params=_cp(("parallel",)),
        name="s5_params",
    )(*args)

    u = p_u.reshape(bsz, nc, lc, g, nh).transpose(3, 0, 1, 2, 4).reshape(g, rows, m)
    z = pl.pallas_call(
        functools.partial(_s5_conv_body, nc=nc, n_levels=n_levels),
        grid=(g,),
        in_specs=[gspec(u), gspec(t_mat), gspec(bmat), gspec(cmat), gspec(q)],
        out_specs=pl.BlockSpec((1, rows, m), lambda i: (i, 0, 0)),
        out_shape=jax.ShapeDtypeStruct((g, rows, m), BF16),
        compiler_params=_cp(("parallel",)),
        name="s5_conv",
    )(u, t_mat, bmat, cmat, q)
    return z.reshape(g, bsz, nc, lc, nh).transpose(1, 2, 3, 0, 4).reshape(bsz, seq, ds)


def _merge_body(yr_ref, z_ref, wp_ref, w1_ref, w2_ref, gr_ref, gs_ref, o_ref):
    y_rwkv = jnp.dot(yr_ref[...], wp_ref[...], preferred_element_type=F32)
    z = z_ref[...]
    y_s5 = (jnp.dot(z, w1_ref[...], preferred_element_type=F32)
            * _sigmoid(jnp.dot(z, w2_ref[...], preferred_element_type=F32)))
    o_ref[...] = (_sigmoid(gr_ref[...].astype(F32)) * y_rwkv
                  + _sigmoid(gs_ref[...].astype(F32)) * y_s5).astype(o_ref.dtype)


def _merge(yr, z, wp, w1, w2, p_gates, *, tm=512, tn=512):
    t, kr = yr.shape
    ks = z.shape[1]
    d = wp.shape[1]
    tm, tn = min(tm, t), min(tn, d)
    nj = d // tn
    return pl.pallas_call(
        _merge_body,
        grid=(t // tm, nj),
        in_specs=[pl.BlockSpec((tm, kr), lambda i, j: (i, 0)),
                  pl.BlockSpec((tm, ks), lambda i, j: (i, 0)),
                  pl.BlockSpec((kr, tn), lambda i, j: (0, j)),
                  pl.BlockSpec((ks, tn), lambda i, j: (0, j)),
                  pl.BlockSpec((ks, tn), lambda i, j: (0, j)),
                  pl.BlockSpec((tm, tn), lambda i, j: (i, j)),
                  pl.BlockSpec((tm, tn), lambda i, j: (i, j + nj))],
        out_specs=pl.BlockSpec((tm, tn), lambda i, j: (i, j)),
        out_shape=jax.ShapeDtypeStruct((t, d), BF16),
        compiler_params=_cp(("parallel", "parallel")),
        name="merge_branches",
    )(yr, z, wp, w1, w2, p_gates, p_gates)


def _router_body(x_ref, mix_ref, g1_ref, lng_ref, lnb_ref, sc_ref, sh_ref, rw_ref, rb_ref,
                 x1_ref, h2_ref, idx_ref, w_ref, *, alpha):
    x1 = _ln(alpha * x_ref[0] + g1_ref[0] * mix_ref[0].astype(F32)) * lng_ref[...] + lnb_ref[...]
    x1_ref[0] = x1
    h2 = _ln(x1) * (1.0 + sc_ref[0]) + sh_ref[0]
    h2_ref[0] = h2.astype(h2_ref.dtype)
    logits = jnp.dot(h2, rw_ref[...], preferred_element_type=F32, precision=HI) + rb_ref[...]
    lane = lax.broadcasted_iota(I32, logits.shape, 1)
    lane_f = lane.astype(F32)
    idx_out = jnp.zeros(logits.shape, I32)
    val_out = jnp.zeros(logits.shape, F32)
    vals = []
    for kth in range(TOP_K):
        mx = jnp.max(logits, axis=-1, keepdims=True)
        sel = jnp.min(jnp.where(logits == mx, lane_f, float(LANES)), axis=-1, keepdims=True).astype(I32)
        vals.append(mx)
        idx_out = jnp.where(lane == kth, sel, idx_out)
        logits = jnp.where(lane == sel, -jnp.inf, logits)
    exps = [jnp.exp(vv - vals[0]) for vv in vals]
    tot = exps[0]
    for e in exps[1:]:
        tot = tot + e
    for kth in range(TOP_K):
        val_out = jnp.where(lane == kth, exps[kth] / tot, val_out)
    idx_ref[0] = idx_out
    w_ref[0] = val_out


def _resid_ln_router(x, mix, mod3, ln_g, ln_b, router_w, router_b, *, alpha, tl=256):
    bsz, seq, d = x.shape
    e = router_w.shape[1]
    tl = min(tl, seq)
    rw = jnp.zeros((d, LANES), F32).at[:, :e].set(router_w.astype(F32))
    rb = jnp.full((1, LANES), -jnp.inf, F32).at[0, :e].set(router_b.astype(F32))
    tok = lambda w: pl.BlockSpec((1, tl, w), lambda b, l: (b, l, 0))
    modspec = lambda j: pl.BlockSpec((1, 1, d), lambda b, l: (b * 6 + j, 0, 0))
    full = lambda a: pl.BlockSpec(a.shape, lambda b, l: (0,) * a.ndim)
    args = (x, mix, mod3, ln_g[None].astype(F32), ln_b[None].astype(F32), mod3, mod3, rw, rb)
    return pl.pallas_call(
        functools.partial(_router_body, alpha=alpha),
        grid=(bsz, seq // tl),
        in_specs=[tok(d), tok(d), modspec(2), full(args[3]), full(args[4]), modspec(4), modspec(3),
                  full(rw), full(rb)],
        out_specs=[tok(d), tok(d), tok(LANES), tok(LANES)],
        out_shape=[jax.ShapeDtypeStruct((bsz, seq, d), F32),
                   jax.ShapeDtypeStruct((bsz, seq, d), BF16),
                   jax.ShapeDtypeStruct((bsz, seq, LANES), I32),
                   jax.ShapeDtypeStruct((bsz, seq, LANES), F32)],
        compiler_params=_cp(("parallel", "parallel")),
        name="resid_ln_router",
    )(*args)


def _gather_body(idx_ref, src_ref, o_ref, sem, *, rows_per_step):
    base = pl.program_id(0) * rows_per_step

    def row_copy(r):
        return pltpu.make_async_copy(src_ref.at[idx_ref[base + r]], o_ref.at[r], sem)

    def start(r, carry):
        row_copy(r).start()
        return carry

    def wait(r, carry):
        row_copy(r).wait()
        return carry

    lax.fori_loop(0, rows_per_step, start, 0, unroll=8)
    lax.fori_loop(0, rows_per_step, wait, 0, unroll=8)


def _gather_rows(src, idx, *, rows_per_step=256):
    n = idx.shape[0]
    width = src.shape[1]
    rows_per_step = min(rows_per_step, n)
    assert n % rows_per_step == 0 and width % LANES == 0
    src = src.reshape(src.shape[0], width // LANES, LANES)
    return pl.pallas_call(
        functools.partial(_gather_body, rows_per_step=rows_per_step),
        grid_spec=pltpu.PrefetchScalarGridSpec(
            num_scalar_prefetch=1,
            grid=(n // rows_per_step,),
            in_specs=[pl.BlockSpec(memory_space=pl.ANY)],
            out_specs=pl.BlockSpec((rows_per_step,) + src.shape[1:], lambda i, idx_ref: (i, 0, 0)),
            scratch_shapes=[pltpu.SemaphoreType.DMA(())],
        ),
        out_shape=jax.ShapeDtypeStruct((n,) + src.shape[1:], src.dtype),
        compiler_params=_cp(("arbitrary",)),
        name="gather_rows",
    )(idx, src).reshape(n, width)


def _expert_up_body(be_ref, x_ref, wg_ref, bg_ref, wu_ref, bu_ref, o_ref):
    x = x_ref[...]
    gate = jnp.dot(x, wg_ref[0], preferred_element_type=F32) + bg_ref[0]
    up = jnp.dot(x, wu_ref[0], preferred_element_type=F32) + bu_ref[0]
    gate = jnp.minimum(gate, SWIGLU_LIMIT)
    up = jnp.clip(up, -SWIGLU_LIMIT, SWIGLU_LIMIT)
    o_ref[...] = ((up + 1.0) * gate * _sigmoid(SWIGLU_ALPHA * gate)).astype(o_ref.dtype)


def _expert_down_body(be_ref, a_ref, wd_ref, bd_ref, rw_ref, o_ref):
    y = jnp.dot(a_ref[...], wd_ref[0], preferred_element_type=F32) + bd_ref[0]
    o_ref[...] = (y * rw_ref[...]).astype(o_ref.dtype)


def _experts(xs, block_e, row_w, w_gate, b_gate, w_up, b_up, w_down, b_down):
    n_rows, d = xs.shape
    e, _, f = w_gate.shape
    nb = n_rows // EXPERT_BLOCK
    rows = lambda w: pl.BlockSpec((EXPERT_BLOCK, w), lambda i, be: (i, 0))
    byexp = lambda s: pl.BlockSpec((1,) + s, lambda i, be: (be[i], 0, 0))
    act = pl.pallas_call(
        _expert_up_body,
        grid_spec=pltpu.PrefetchScalarGridSpec(
            num_scalar_prefetch=1, grid=(nb,),
            in_specs=[rows(d), byexp((d, f)), byexp((1, f)), byexp((d, f)), byexp((1, f))],
            out_specs=rows(f)),
        out_shape=jax.ShapeDtypeStruct((n_rows, f), BF16),
        compiler_params=_cp(("arbitrary",)),
        name="expert_up",
    )(block_e, xs, w_gate.astype(BF16), b_gate[:, None, :].astype(F32),
      w_up.astype(BF16), b_up[:, None, :].astype(F32))
    return pl.pallas_call(
        _expert_down_body,
        grid_spec=pltpu.PrefetchScalarGridSpec(
            num_scalar_prefetch=1, grid=(nb,),
            in_specs=[rows(f), byexp((f, d)), byexp((1, d)), rows(1)],
            out_specs=rows(d)),
        out_shape=jax.ShapeDtypeStruct((n_rows, d), BF16),
        compiler_params=_cp(("arbitrary",)),
        name="expert_down",
    )(block_e, act, w_down.astype(BF16), b_down[:, None, :].astype(F32), row_w[:, None])


def _final_body(x1_ref, y_ref, g2_ref, lng_ref, lnb_ref, o_ref, *, alpha):
    ff = y_ref[0].astype(F32)
    for kth in range(1, TOP_K):
        ff = ff + y_ref[kth].astype(F32)
    o_ref[0] = (_ln(alpha * x1_ref[0] + g2_ref[0] * ff) * lng_ref[...] + lnb_ref[...]).astype(o_ref.dtype)


def _final(x1, ysel, mod3, ln_g, ln_b, out_dtype, *, alpha, tl=256):
    bsz, seq, d = x1.shape
    tl = min(tl, seq)
    nl = seq // tl
    full = lambda a: pl.BlockSpec(a.shape, lambda b, l: (0,) * a.ndim)
    lg, lb = ln_g[None].astype(F32), ln_b[None].astype(F32)
    return pl.pallas_call(
        functools.partial(_final_body, alpha=alpha),
        grid=(bsz, nl),
        in_specs=[pl.BlockSpec((1, tl, d), lambda b, l: (b, l, 0)),
                  pl.BlockSpec((TOP_K, tl, d), lambda b, l: (0, b * nl + l, 0)),
                  pl.BlockSpec((1, 1, d), lambda b, l: (b * 6 + 5, 0, 0)),
                  full(lg), full(lb)],
        out_specs=pl.BlockSpec((1, tl, d), lambda b, l: (b, l, 0)),
        out_shape=jax.ShapeDtypeStruct((bsz, seq, d), out_dtype),
        compiler_params=_cp(("parallel", "parallel")),
        name="combine_ln",
    )(x1, ysel, mod3, lg, lb)


def _route(top_idx, top_w, n_experts):
    n_tok = top_idx.shape[0]
    n_assign = n_tok * TOP_K
    flat_e = top_idx.reshape(n_assign)
    flat_tok = jnp.arange(n_assign, dtype=I32) // TOP_K
    flat_w = top_w.reshape(n_assign)
    order = jnp.argsort(flat_e)
    sorted_e = flat_e[order]
    counts = jnp.bincount(flat_e, length=n_experts)
    padded = (counts + EXPERT_BLOCK - 1) // EXPERT_BLOCK * EXPERT_BLOCK
    start = jnp.cumsum(counts) - counts
    pad_end = jnp.cumsum(padded)
    pad_start = pad_end - padded
    dest = (pad_start[sorted_e] + jnp.arange(n_assign, dtype=I32) - start[sorted_e]).astype(I32)
    n_blocks = -(-(n_assign + n_experts * (EXPERT_BLOCK - 1)) // EXPERT_BLOCK)
    n_rows = n_blocks * EXPERT_BLOCK
    row_tok = jnp.zeros((n_rows,), I32).at[dest].set(flat_tok[order])
    row_w = jnp.zeros((n_rows,), F32).at[dest].set(flat_w[order])
    block_start = jnp.arange(n_blocks, dtype=pad_end.dtype) * EXPERT_BLOCK
    block_e = jnp.minimum(jnp.searchsorted(pad_end, block_start, side='right'),
                          n_experts - 1).astype(I32)
    pos = jnp.zeros((n_assign,), I32).at[order].set(dest)
    pos_kmajor = pos.reshape(n_tok, TOP_K).T.reshape(n_assign)
    return row_tok, row_w, block_e, pos_kmajor


def kernel(x, c, ada_w, ada_b, w_in, shift_mu, rwkv_w0, rwkv_w2, rwkv_a0, rwkv_a2, rwkv_g2,
           rwkv_k_k, rwkv_k_a, rwkv_r_k, rwkv_ln_g, rwkv_ln_b, rwkv_proj,
           s5_a_re, s5_a_im, s5_log_dt, s5_b_re, s5_b_im, s5_c_re, s5_c_im, s5_d,
           s5_glu_w1, s5_glu_w2, w_out, ln1_g, ln1_b, ln2_g, ln2_b, router_w, router_b,
           exp_w_gate, exp_b_gate, exp_w_up, exp_b_up, exp_w_down, exp_b_down):
    out_dtype = x.dtype
    bsz, seq, d = x.shape
    depth = ada_w.shape[0]
    alpha = (2 * depth) ** 0.25
    dr = rwkv_w0.shape[1]
    ds = s5_glu_w1.shape[1]
    shift_w = shift_mu.shape[1]
    n_tok = bsz * seq
    n_experts = router_w.shape[2]
    x = x.astype(F32)
    c_pad = jnp.zeros((8, d), F32).at[:bsz].set(c.astype(F32))

    for l in range(depth):
        mod = _matmul(c_pad, ada_w[l], ada_b[l][None].astype(F32), out_dtype=F32, tm=8, tn=512,
                      name="adaln", silu_a=True)[:bsz]
        mod3 = mod.reshape(bsz * 6, 1, d)

        h = _ln_mod(x, mod3, 1, 0).reshape(n_tok, d)
        wi = w_in[l].astype(BF16)
        proj = lambda lo, hi, tn, name: _matmul(h, wi[:, lo:hi], out_dtype=BF16, tm=1024, tn=tn,
                                                name=name)
        p_rkv = proj(0, 3 * dr, 512, "proj_rkv").reshape(bsz, seq, 3 * dr)
        p_lora = proj(3 * dr, shift_w, 256, "proj_lora").reshape(bsz, seq, shift_w - 3 * dr)
        p_u = proj(shift_w, shift_w + ds, 512, "proj_s5").reshape(bsz, seq, ds)
        p_gates = proj(shift_w + ds, shift_w + ds + 2 * d, 512, "proj_gates")

        r, k, v, a, lw, g = _rwkv_prep(p_rkv, p_lora, shift_mu[l].astype(F32), rwkv_w0[l].astype(F32),
                                       rwkv_w2[l], rwkv_a0[l].astype(F32), rwkv_a2[l], rwkv_g2[l])
        yr = _rwkv_recurrence(r, k, v, a, lw, g, rwkv_k_k[l], rwkv_k_a[l], rwkv_r_k[l],
                              rwkv_ln_g[l], rwkv_ln_b[l])
        z = _s5_branch(p_u, s5_a_re[l], s5_a_im[l], s5_log_dt[l], s5_b_re[l], s5_b_im[l],
                       s5_c_re[l], s5_c_im[l], s5_d[l])
        merged = _merge(yr.reshape(n_tok, dr), z.reshape(n_tok, ds), rwkv_proj[l].astype(BF16),
                        s5_glu_w1[l].astype(BF16), s5_glu_w2[l].astype(BF16), p_gates)
        mix = _matmul(merged, w_out[l].astype(BF16), out_dtype=F32, tm=1024, tn=512, name="w_out")

        x1, h2, top_idx, top_w = _resid_ln_router(x, mix.reshape(bsz, seq, d), mod3, ln1_g[l], ln1_b[l],
                                                  router_w[l], router_b[l], alpha=alpha)
        row_tok, row_w, block_e, pos_kmajor = _route(
            top_idx.reshape(n_tok, LANES)[:, :TOP_K], top_w.reshape(n_tok, LANES)[:, :TOP_K], n_experts)
        xs = _gather_rows(h2.reshape(n_tok, d), row_tok)
        ys = _experts(xs, block_e, row_w, exp_w_gate[l], exp_b_gate[l], exp_w_up[l], exp_b_up[l],
                      exp_w_down[l], exp_b_down[l])
        ysel = _gather_rows(ys, pos_kmajor).reshape(TOP_K, n_tok, d)
        x = _final(x1, ysel, mod3, ln2_g[l], ln2_b[l], F32, alpha=alpha)
    return x.astype(out_dtype)
```

```python
import functools

import jax
import jax.numpy as jnp
from jax import lax
from jax.experimental import pallas as pl
from jax.experimental.pallas import tpu as pltpu

F32 = jnp.float32
BF16 = jnp.bfloat16
I32 = jnp.int32
U32 = jnp.uint32

LN_EPS = 1e-5
RWKV_GN_EPS = 64e-5
TOP_K = 4
EXPERT_BLOCK = 256
SWIGLU_LIMIT = 7.0
SWIGLU_ALPHA = 1.702
RWKV_CHUNK = 64
S5_CHUNK = 64
LANES = 128
VMEM_LIMIT = 56 * 1024 * 1024
HI = lax.Precision.HIGHEST


def _cp(sem, vmem=VMEM_LIMIT):
    return pltpu.CompilerParams(dimension_semantics=sem, vmem_limit_bytes=vmem)


def _dot(a, b):
    return jnp.dot(a.astype(BF16), b.astype(BF16), preferred_element_type=F32)


def _dot_nt(a, b, precision=None):
    return lax.dot_general(a, b, (((1,), (1,)), ((), ())), preferred_element_type=F32,
                           precision=precision)


def _dot_tn(a, b):
    return lax.dot_general(a.astype(BF16), b.astype(BF16), (((0,), (0,)), ((), ())),
                           preferred_element_type=F32)


def _ln(x):
    mean = jnp.mean(x, axis=-1, keepdims=True)
    xc = x - mean
    var = jnp.mean(xc * xc, axis=-1, keepdims=True)
    return xc * lax.rsqrt(var + LN_EPS)


def _sigmoid(x):
    return 1.0 / (1.0 + jnp.exp(-x))


def _tile(n, pref, align):
    if n <= pref:
        return n
    t = pref - pref % align
    while n % t:
        t -= align
    return t


def _pack_rows(x, o_ref):
    half = x.shape[1] // 2
    lo = lax.bitcast_convert_type(x[:, :half].astype(BF16).astype(F32), U32) >> 16
    hi = lax.bitcast_convert_type(x[:, half:].astype(BF16).astype(F32), U32) & jnp.uint32(0xFFFF0000)
    words = hi | lo
    for s in range(half // LANES):
        o_ref[:, s, :] = words[:, s * LANES:(s + 1) * LANES]


def _unpack_rows(x_ref):
    los, his = [], []
    for s in range(x_ref.shape[1]):
        words = x_ref[:, s, :]
        los.append(lax.bitcast_convert_type(words << 16, F32))
        his.append(lax.bitcast_convert_type(words & jnp.uint32(0xFFFF0000), F32))
    return jnp.concatenate(los + his, axis=-1)


def _mm_body(a_ref, b_ref, bias_ref, o_ref, *, silu_a):
    a = a_ref[...]
    if silu_a:
        a = a * _sigmoid(a)
    o_ref[...] = (_dot(a, b_ref[...]) + bias_ref[...]).astype(o_ref.dtype)


def _matmul(a, b, bias=None, *, out_dtype, tm, tn, name, silu_a=False):
    m, k = a.shape
    n = b.shape[1]
    tm, tn = _tile(m, tm, 8), _tile(n, tn, LANES)
    if bias is None:
        bias = jnp.zeros((1, n), F32)
    return pl.pallas_call(
        functools.partial(_mm_body, silu_a=silu_a),
        grid=(m // tm, n // tn),
        in_specs=[pl.BlockSpec((tm, k), lambda i, j: (i, 0)),
                  pl.BlockSpec((k, tn), lambda i, j: (0, j)),
                  pl.BlockSpec((1, tn), lambda i, j: (0, j))],
        out_specs=pl.BlockSpec((tm, tn), lambda i, j: (i, j)),
        out_shape=jax.ShapeDtypeStruct((m, n), out_dtype),
        compiler_params=_cp(("parallel", "parallel")),
        name=name,
    )(a, b, bias)


def _ln_mod_body(x_ref, sc_ref, sh_ref, o_ref):
    y = _ln(x_ref[0])
    o_ref[0] = (y * (1.0 + sc_ref[0]) + sh_ref[0]).astype(o_ref.dtype)


def _ln_mod(x, mod3, sc_idx, sh_idx, *, tl=256):
    bsz, seq, d = x.shape
    tl = min(tl, seq)
    return pl.pallas_call(
        _ln_mod_body,
        grid=(bsz, seq // tl),
        in_specs=[pl.BlockSpec((1, tl, d), lambda b, l: (b, l, 0)),
                  pl.BlockSpec((1, 1, d), lambda b, l: (b * 6 + sc_idx, 0, 0)),
                  pl.BlockSpec((1, 1, d), lambda b, l: (b * 6 + sh_idx, 0, 0))],
        out_specs=pl.BlockSpec((1, tl, d), lambda b, l: (b, l, 0)),
        out_shape=jax.ShapeDtypeStruct((bsz, seq, d), BF16),
        compiler_params=_cp(("parallel", "parallel")),
        name="ln_modulate",
    )(x, mod3, mod3)


def _token_shift(x, carry_ref, mu):
    rolled = pltpu.roll(x, 1, 0)
    row = lax.broadcasted_iota(I32, x.shape, 0)
    prev = jnp.where(row == 0, carry_ref[...], rolled)
    carry_ref[...] = x[x.shape[0] - 1:, :]
    return x + (prev - x) * mu


def _rwkv_prep_body(prkv_ref, plora_ref, mu_rkv_ref, mu_lora_ref, w0_ref, w2_ref, a0_ref, a2_ref,
                    g2_ref, r_ref, k_ref, v_ref, a_ref, lw_ref, g_ref, carry_rkv, carry_lora,
                    *, dr, w_lora, a_lora):
    @pl.when(pl.program_id(1) == 0)
    def _():
        carry_rkv[...] = jnp.zeros_like(carry_rkv)
        carry_lora[...] = jnp.zeros_like(carry_lora)

    xs = _token_shift(prkv_ref[0].astype(F32), carry_rkv, mu_rkv_ref[...])
    r_ref[0] = xs[:, :dr].astype(r_ref.dtype)
    k_ref[0] = xs[:, dr:2 * dr].astype(k_ref.dtype)
    v_ref[0] = xs[:, 2 * dr:].astype(v_ref.dtype)

    ls = _token_shift(plora_ref[0].astype(F32), carry_lora, mu_lora_ref[...])
    xw = ls[:, :w_lora]
    xa = ls[:, w_lora:w_lora + a_lora]
    xg = ls[:, w_lora + a_lora:]
    z = -(w0_ref[...] + _dot(jnp.tanh(xw), w2_ref[...]))
    softplus = jnp.maximum(z, 0.0) + jnp.log1p(jnp.exp(-jnp.abs(z)))
    lw_ref[0] = -jnp.exp(-softplus - 0.5)
    a_ref[0] = _sigmoid(a0_ref[...] + _dot(xa, a2_ref[...])).astype(a_ref.dtype)
    g_ref[0] = _dot(_sigmoid(xg), g2_ref[...]).astype(g_ref.dtype)


def _rwkv_prep(p_rkv, p_lora, mu, w0, w2, a0, a2, g2, *, tl=256):
    bsz, seq, dr3 = p_rkv.shape
    dr = dr3 // 3
    lora = p_lora.shape[-1]
    w_lora, a_lora = w2.shape[0], a2.shape[0]
    tl = min(tl, seq)
    tok = lambda w: pl.BlockSpec((1, tl, w), lambda b, l: (b, l, 0))
    full = lambda a: pl.BlockSpec(a.shape, lambda b, l: (0,) * a.ndim)
    mu_rkv, mu_lora = mu[None, :dr3], mu[None, dr3:]
    args = (p_rkv, p_lora, mu_rkv, mu_lora, w0[None], w2.astype(BF16), a0[None], a2.astype(BF16),
            g2.astype(BF16))
    out = jax.ShapeDtypeStruct((bsz, seq, dr), BF16)
    return pl.pallas_call(
        functools.partial(_rwkv_prep_body, dr=dr, w_lora=w_lora, a_lora=a_lora),
        grid=(bsz, seq // tl),
        in_specs=[tok(dr3), tok(lora)] + [full(a) for a in args[2:]],
        out_specs=[tok(dr)] * 6,
        out_shape=[out, out, out, out, jax.ShapeDtypeStruct((bsz, seq, dr), F32), out],
        scratch_shapes=[pltpu.VMEM((1, dr3), F32), pltpu.VMEM((1, lora), F32)],
        compiler_params=_cp(("parallel", "arbitrary")),
        name="rwkv_prep",
    )(*args)


def _split_bf16(x):
    hi = x.astype(BF16)
    return hi, (x - hi.astype(F32)).astype(BF16)


def _head_sum(x, bd_ref):
    c, w = x.shape
    gw = bd_ref.shape[0]
    ng = w // gw
    hi, lo = _split_bf16(x)
    groups = [p[:, j * gw:(j + 1) * gw] for p in (hi, lo) for j in range(ng)]
    res = jnp.dot(jnp.concatenate(groups, axis=0), bd_ref[...], preferred_element_type=F32)
    parts = [res[j * c:(j + 1) * c] + res[(ng + j) * c:(ng + j + 1) * c] for j in range(ng)]
    return jnp.concatenate(parts, axis=-1)


def _rwkv_rec_body(r_ref, k_ref, v_ref, a_ref, lw_ref, g_ref, kk_ref, ka_ref, rk_ref, lng_ref,
                   lnb_ref, bd_ref, y_ref, state, ybuf, *, heads, n):
    c = r_ref.shape[1]

    @pl.when(pl.program_id(2) == 0)
    def _():
        state[...] = jnp.zeros_like(state)

    row = lax.broadcasted_iota(I32, (c, c), 0)
    col = lax.broadcasted_iota(I32, (c, c), 1)
    strict = row > col
    eye = (row == col).astype(F32)
    tril_b = (row >= col).astype(BF16)
    row2 = lax.broadcasted_iota(I32, (c, 2 * c), 0)
    col2 = lax.broadcasted_iota(I32, (c, 2 * c), 1)
    strict_right = (col2 >= c) & (row2 > col2 - c)
    lower_pair = row2 >= jnp.where(col2 >= c, col2 - c, col2)

    r = r_ref[0].astype(F32)
    k = k_ref[0].astype(F32)
    v = v_ref[0].astype(F32)
    a = a_ref[0].astype(F32)
    lw = lw_ref[0]
    lw_hi, lw_lo = _split_bf16(lw)
    cl = (jnp.dot(tril_b, lw_hi, preferred_element_type=F32)
          + jnp.dot(tril_b, lw_lo, preferred_element_type=F32))
    w_inc = jnp.exp(cl)
    w_inv = jnp.exp(-cl)
    w_prev = jnp.exp(cl - lw)
    kk = k * kk_ref[...]
    kk = kk / jnp.maximum(jnp.sqrt(_head_sum(kk * kk, bd_ref)), 1e-12)
    k2 = k * (1.0 + (a - 1.0) * ka_ref[...])
    rt = (r * w_inc).astype(BF16)
    kkt = (kk * w_prev).astype(BF16)
    bt = (kk * a * w_inv).astype(BF16)
    kt = (k2 * w_inv).astype(BF16)
    vb = v.astype(BF16)
    bonus = _head_sum(r * k2 * rk_ref[...], bd_ref) * v

    hs = range(heads)
    sls = [slice(h * n, (h + 1) * n) for h in hs]
    lhs2 = [jnp.concatenate([kkt[:, sl], rt[:, sl]], axis=0) for sl in sls]
    rhs2 = [jnp.concatenate([bt[:, sl], kt[:, sl]], axis=0) for sl in sls]
    vh = [vb[:, sl] for sl in sls]
    s0 = [state[h] for h in hs]
    gram = [_dot_nt(lhs2[h], rhs2[h]) for h in hs]
    m1 = [_dot_nt(lhs2[h], s0[h].astype(BF16)) for h in hs]
    l_b = [jnp.where(strict, gram[h][:c, :c], 0.0) for h in hs]
    lk_only = [jnp.where(strict_right, gram[h][:c], 0.0).astype(BF16) for h in hs]
    p_bk = [jnp.where(lower_pair, gram[h][c:], 0.0).astype(BF16) for h in hs]
    rhs_u = [m1[h][:c] + jnp.dot(lk_only[h], jnp.concatenate([vh[h], vh[h]], axis=0),
                                 preferred_element_type=F32) for h in hs]

    t_inv = [eye - l_b[h] for h in hs]
    pw = [_dot(l_b[h], l_b[h]) for h in hs]
    span = 4
    while span < c:
        both = [_dot(jnp.concatenate([t_inv[h], pw[h]], axis=0), pw[h]) for h in hs]
        t_inv = [t_inv[h] + both[h][:c] for h in hs]
        pw = [both[h][c:] for h in hs]
        span *= 2
    t_inv = [t_inv[h] + _dot(t_inv[h], pw[h]) for h in hs]

    u = [-_dot(t_inv[h], rhs_u[h]) for h in hs]
    uv = [jnp.concatenate([u[h].astype(BF16), vh[h]], axis=0) for h in hs]
    y_h = [m1[h][c:] + jnp.dot(p_bk[h], uv[h], preferred_element_type=F32) for h in hs]
    s_new = [s0[h] + _dot_tn(uv[h], rhs2[h]) for h in hs]
    for h in hs:
        ybuf[:, sls[h]] = y_h[h]
        state[h] = s_new[h] * w_inc[c - 1:, sls[h]]

    y = ybuf[...]
    inv_n = 1.0 / n
    yc = y - _head_sum(y, bd_ref) * inv_n
    yv = _head_sum(yc * yc, bd_ref) * inv_n
    yn = yc * lax.rsqrt(yv + RWKV_GN_EPS) * lng_ref[...] + lnb_ref[...]
    y_ref[0] = ((yn + bonus) * g_ref[0].astype(F32)).astype(y_ref.dtype)


def _rwkv_recurrence(r, k, v, a, lw, g, k_k, k_a, r_k, ln_g, ln_b, *, heads_per_step=16):
    bsz, seq, dr = r.shape
    n = k_k.shape[-1]
    hb = min(heads_per_step, dr // n)
    wblk = hb * n
    gw = min(2 * LANES, wblk)
    assert wblk % LANES == 0 and dr % wblk == 0 and seq % RWKV_CHUNK == 0 and wblk % gw == 0
    c = RWKV_CHUNK
    tok = pl.BlockSpec((1, c, wblk), lambda b, h, t: (b, t, h))
    par = pl.BlockSpec((1, wblk), lambda b, h, t: (0, h))
    flat = lambda p: p.reshape(1, dr).astype(F32)
    gi = lax.broadcasted_iota(I32, (gw, gw), 0) // n
    gj = lax.broadcasted_iota(I32, (gw, gw), 1) // n
    block_diag = (gi == gj).astype(BF16)
    return pl.pallas_call(
        functools.partial(_rwkv_rec_body, heads=hb, n=n),
        grid=(bsz, dr // wblk, seq // c),
        in_specs=[tok] * 6 + [par] * 5 + [pl.BlockSpec((gw, gw), lambda b, h, t: (0, 0))],
        out_specs=tok,
        out_shape=jax.ShapeDtypeStruct((bsz, seq, dr), BF16),
        scratch_shapes=[pltpu.VMEM((hb, n, n), F32), pltpu.VMEM((c, wblk), F32)],
        compiler_params=_cp(("parallel", "parallel", "arbitrary")),
        name="rwkv_recurrence",
    )(r, k, v, a, lw, g, flat(k_k), flat(k_a), flat(r_k), flat(ln_g), flat(ln_b), block_diag)


def _s5_param_body(are_ref, aim_ref, ldt_ref, bre_ref, bim_ref, cre_ref, cim_ref, d_ref,
                   t_ref, bmat_ref, cmat_ref, q_ref, *, lc, n_levels):
    a_re, a_im = are_ref[0], aim_ref[0]
    dt = jnp.exp(ldt_ref[0])
    lam = a_re * dt
    ang = a_im * dt
    mag = jnp.exp(lam)
    ab_re, ab_im = mag * jnp.cos(ang), mag * jnp.sin(ang)
    den = a_re * a_re + a_im * a_im
    f_re = ((ab_re - 1.0) * a_re + ab_im * a_im) / den
    f_im = (ab_im * a_re - (ab_re - 1.0) * a_im) / den
    b_re, b_im = bre_ref[0], bim_ref[0]
    bb_re = f_re * b_re - f_im * b_im
    bb_im = f_re * b_im + f_im * b_re
    c_re, c_im = cre_ref[0], cim_ref[0]
    nh, p2 = b_re.shape
    lo = lax.broadcasted_iota(I32, (1, p2), 1) < (p2 // 2)

    def power(tf):
        m = jnp.exp(tf * lam)
        return m * jnp.cos(tf * ang), m * jnp.sin(tf * ang)

    def cmul3(xr, xi, pr, pi):
        rr = xr[None] * pr[:, None, :] - xi[None] * pi[:, None, :]
        ri = xr[None] * pi[:, None, :] + xi[None] * pr[:, None, :]
        return rr.reshape(lc * nh, p2), ri.reshape(lc * nh, p2)

    tcol = lax.broadcasted_iota(I32, (lc, 1), 0).astype(F32)
    nr, ni = power(-tcol)
    pr, pi = power(tcol)
    bm_re, bm_im = cmul3(bb_re, bb_im, nr, ni)
    cm_re, cm_im = cmul3(c_re, c_im, pr, pi)
    x = jnp.where(lo, bm_re, -bm_im)
    yc = jnp.where(lo, cm_re, cm_im)
    t = _dot_nt(x, yc, precision=HI)
    m = lc * nh
    ri = lax.broadcasted_iota(I32, (m, m), 0)
    ci = lax.broadcasted_iota(I32, (m, m), 1)
    t = jnp.where(ci // nh >= ri // nh, t, 0.0)
    t = jnp.where(ri == ci, t + d_ref[0], t)
    t_ref[0] = t.astype(t_ref.dtype)

    er, ei = power(jnp.full((1, 1), lc - 1.0, F32))
    bmat_ref[0] = jnp.where(lo, bm_re * er - bm_im * ei, bm_re * ei + bm_im * er).astype(bmat_ref.dtype)
    cmat_ref[0] = jnp.where(lo, cm_re * ab_re - cm_im * ab_im,
                            -(cm_re * ab_im + cm_im * ab_re)).astype(cmat_ref.dtype)
    for i in range(n_levels):
        qr, qi = power(jnp.full((1, 1), float(lc * 2 ** i), F32))
        q_ref[0, i:i + 1, :] = qr
        q_ref[0, n_levels + i:n_levels + i + 1, :] = jnp.where(lo, -qi, qi)


def _s5_conv_body(u_ref, t_ref, bmat_ref, cmat_ref, q_ref, z_ref, *, nc, n_levels):
    u = u_ref[0]
    zloc = jnp.dot(u, bmat_ref[0], preferred_element_type=F32)
    rows, p2 = zloc.shape
    n_in_seq = lax.broadcasted_iota(I32, (rows, p2), 0) % nc
    s = jnp.where(n_in_seq >= 1, pltpu.roll(zloc, 1, 0), 0.0)
    for i in range(n_levels):
        step = 2 ** i
        sh = jnp.where(n_in_seq >= step, pltpu.roll(s, step, 0), 0.0)
        sw = pltpu.roll(sh, p2 // 2, 1)
        s = s + q_ref[0, i:i + 1, :] * sh + q_ref[0, n_levels + i:n_levels + i + 1, :] * sw
    y = jnp.dot(u, t_ref[0], preferred_element_type=F32) + _dot_nt(s.astype(BF16), cmat_ref[0])
    z_ref[0] = jax.nn.gelu(y).astype(z_ref.dtype)


def _s5_branch(p_u, a_re, a_im, log_dt, b_re, b_im, c_re, c_im, d):
    bsz, seq, ds = p_u.shape
    g, p = a_re.shape
    nh = ds // g
    lc = min(S5_CHUNK, seq)
    nc = seq // lc
    n_levels = max(1, (nc - 1).bit_length())
    rows = bsz * nc
    m = lc * nh
    tile2 = lambda a: jnp.concatenate([a, a], axis=-1).astype(F32)
    args = (tile2(a_re)[:, None, :], tile2(a_im)[:, None, :], log_dt.reshape(g, 1, 1).astype(F32),
            tile2(jnp.swapaxes(b_re, 1, 2)), tile2(jnp.swapaxes(b_im, 1, 2)),
            tile2(c_re), tile2(c_im), jnp.tile(d.astype(F32), (1, lc))[:, None, :])
    gspec = lambda a: pl.BlockSpec((1,) + a.shape[1:], lambda i: (i, 0, 0))
    qrows = 2 * n_levels
    t_mat, bmat, cmat, q = pl.pallas_call(
        functools.partial(_s5_param_body, lc=lc, n_levels=n_levels),
        grid=(g,),
        in_specs=[gspec(a) for a in args],
        out_specs=[pl.BlockSpec((1, m, m), lambda i: (i, 0, 0)),
                   pl.BlockSpec((1, m, 2 * p), lambda i: (i, 0, 0)),
                   pl.BlockSpec((1, m, 2 * p), lambda i: (i, 0, 0)),
                   pl.BlockSpec((1, qrows, 2 * p), lambda i: (i, 0, 0))],
        out_shape=[jax.ShapeDtypeStruct((g, m, m), BF16),
                   jax.ShapeDtypeStruct((g, m, 2 * p), BF16),
                   jax.ShapeDtypeStruct((g, m, 2 * p), BF16),
                   jax.ShapeDtypeStruct((g, qrows, 2 * p), F32)],
        compiler_params=_cp(("parallel",)),
        name="s5_params",
    )(*args)

    u = p_u.reshape(bsz, nc, lc, g, nh).transpose(3, 0, 1, 2, 4).reshape(g, rows, m)
    z = pl.pallas_call(
        functools.partial(_s5_conv_body, nc=nc, n_levels=n_levels),
        grid=(g,),
        in_specs=[gspec(u), gspec(t_mat), gspec(bmat), gspec(cmat), gspec(q)],
        out_specs=pl.BlockSpec((1, rows, m), lambda i: (i, 0, 0)),
        out_shape=jax.ShapeDtypeStruct((g, rows, m), BF16),
        compiler_params=_cp(("parallel",)),
        name="s5_conv",
    )(u, t_mat, bmat, cmat, q)
    return z.reshape(g, bsz, nc, lc, nh).transpose(1, 2, 3, 0, 4).reshape(bsz, seq, ds)


def _merge_body(yr_ref, z_ref, wp_ref, w1_ref, w2_ref, gr_ref, gs_ref, o_ref):
    y_rwkv = jnp.dot(yr_ref[...], wp_ref[...], preferred_element_type=F32)
    z = z_ref[...]
    y_s5 = (jnp.dot(z, w1_ref[...], preferred_element_type=F32)
            * _sigmoid(jnp.dot(z, w2_ref[...], preferred_element_type=F32)))
    o_ref[...] = (_sigmoid(gr_ref[...].astype(F32)) * y_rwkv
                  + _sigmoid(gs_ref[...].astype(F32)) * y_s5).astype(o_ref.dtype)


def _merge(yr, z, wp, w1, w2, p_gates, *, tm=512, tn=512):
    t, kr = yr.shape
    ks = z.shape[1]
    d = wp.shape[1]
    tm, tn = min(tm, t), min(tn, d)
    nj = d // tn
    return pl.pallas_call(
        _merge_body,
        grid=(t // tm, nj),
        in_specs=[pl.BlockSpec((tm, kr), lambda i, j: (i, 0)),
                  pl.BlockSpec((tm, ks), lambda i, j: (i, 0)),
                  pl.BlockSpec((kr, tn), lambda i, j: (0, j)),
                  pl.BlockSpec((ks, tn), lambda i, j: (0, j)),
                  pl.BlockSpec((ks, tn), lambda i, j: (0, j)),
                  pl.BlockSpec((tm, tn), lambda i, j: (i, j)),
                  pl.BlockSpec((tm, tn), lambda i, j: (i, j + nj))],
        out_specs=pl.BlockSpec((tm, tn), lambda i, j: (i, j)),
        out_shape=jax.ShapeDtypeStruct((t, d), BF16),
        compiler_params=_cp(("parallel", "parallel")),
        name="merge_branches",
    )(yr, z, wp, w1, w2, p_gates, p_gates)


def _router_body(x_ref, mix_ref, g1_ref, lng_ref, lnb_ref, sc_ref, sh_ref, rw_ref, rb_ref,
                 x1_ref, h2_ref, idx_ref, w_ref, *, alpha):
    x1 = _ln(alpha * x_ref[0] + g1_ref[0] * mix_ref[0].astype(F32)) * lng_ref[...] + lnb_ref[...]
    x1_ref[0] = x1
    h2 = _ln(x1) * (1.0 + sc_ref[0]) + sh_ref[0]
    _pack_rows(h2, h2_ref.at[0])
    logits = jnp.dot(h2, rw_ref[...], preferred_element_type=F32, precision=HI) + rb_ref[...]
    lane = lax.broadcasted_iota(I32, logits.shape, 1)
    lane_f = lane.astype(F32)
    idx_out = jnp.zeros(logits.shape, I32)
    val_out = jnp.zeros(logits.shape, F32)
    vals = []
    for kth in range(TOP_K):
        mx = jnp.max(logits, axis=-1, keepdims=True)
        sel = jnp.min(jnp.where(logits == mx, lane_f, float(LANES)), axis=-1, keepdims=True).astype(I32)
        vals.append(mx)
        idx_out = jnp.where(lane == kth, sel, idx_out)
        logits = jnp.where(lane == sel, -jnp.inf, logits)
    exps = [jnp.exp(vv - vals[0]) for vv in vals]
    tot = exps[0]
    for e in exps[1:]:
        tot = tot + e
    for kth in range(TOP_K):
        val_out = jnp.where(lane == kth, exps[kth] / tot, val_out)
    idx_ref[0] = idx_out
    w_ref[0] = val_out


def _resid_ln_router(x, mix, mod3, ln_g, ln_b, router_w, router_b, *, alpha, tl=256):
    bsz, seq, d = x.shape
    e = router_w.shape[1]
    tl = min(tl, seq)
    rw = jnp.zeros((d, LANES), F32).at[:, :e].set(router_w.astype(F32))
    rb = jnp.full((1, LANES), -jnp.inf, F32).at[0, :e].set(router_b.astype(F32))
    tok = lambda w: pl.BlockSpec((1, tl, w), lambda b, l: (b, l, 0))
    modspec = lambda j: pl.BlockSpec((1, 1, d), lambda b, l: (b * 6 + j, 0, 0))
    full = lambda a: pl.BlockSpec(a.shape, lambda b, l: (0,) * a.ndim)
    args = (x, mix, mod3, ln_g[None].astype(F32), ln_b[None].astype(F32), mod3, mod3, rw, rb)
    return pl.pallas_call(
        functools.partial(_router_body, alpha=alpha),
        grid=(bsz, seq // tl),
        in_specs=[tok(d), tok(d), modspec(2), full(args[3]), full(args[4]), modspec(4), modspec(3),
                  full(rw), full(rb)],
        out_specs=[tok(d), pl.BlockSpec((1, tl, d // (2 * LANES), LANES), lambda b, l: (b, l, 0, 0)),
                   tok(LANES), tok(LANES)],
        out_shape=[jax.ShapeDtypeStruct((bsz, seq, d), F32),
                   jax.ShapeDtypeStruct((bsz, seq, d // (2 * LANES), LANES), U32),
                   jax.ShapeDtypeStruct((bsz, seq, LANES), I32),
                   jax.ShapeDtypeStruct((bsz, seq, LANES), F32)],
        compiler_params=_cp(("parallel", "parallel")),
        name="resid_ln_router",
    )(*args)


def _gather_body(idx_ref, src_ref, o_ref, buf, gather_sem, write_sem, *, rows_per_step, n_steps):
    def row_copy(step, slot, r):
        return pltpu.make_async_copy(src_ref.at[idx_ref[step * rows_per_step + r]],
                                     buf.at[slot, r], gather_sem.at[slot])

    def start_gather(step, slot):
        def body(r, carry):
            row_copy(step, slot, r).start()
            return carry
        lax.fori_loop(0, rows_per_step, body, 0, unroll=8)

    def wait_gather(step, slot):
        def body(r, carry):
            row_copy(step, slot, r).wait()
            return carry
        lax.fori_loop(0, rows_per_step, body, 0, unroll=8)

    def write_back(step, slot):
        return pltpu.make_async_copy(buf.at[slot], o_ref.at[pl.ds(step * rows_per_step, rows_per_step)],
                                     write_sem.at[slot])

    i = pl.program_id(0)
    slot = i % 2
    other = 1 - slot

    @pl.when(i == 0)
    def _():
        start_gather(0, 0)

    @pl.when(i >= 1)
    def _():
        write_back(i - 1, other).wait()

    @pl.when(i + 1 < n_steps)
    def _():
        start_gather(i + 1, other)

    wait_gather(i, slot)
    write_back(i, slot).start()

    @pl.when(i == n_steps - 1)
    def _():
        write_back(i, slot).wait()


def _gather_rows(src, idx, *, rows_per_step=256):
    n = idx.shape[0]
    rows_per_step = min(rows_per_step, n)
    assert n % rows_per_step == 0 and src.ndim == 3 and src.shape[2] == LANES
    return pl.pallas_call(
        functools.partial(_gather_body, rows_per_step=rows_per_step, n_steps=n // rows_per_step),
        grid_spec=pltpu.PrefetchScalarGridSpec(
            num_scalar_prefetch=1,
            grid=(n // rows_per_step,),
            in_specs=[pl.BlockSpec(memory_space=pl.ANY)],
            out_specs=pl.BlockSpec(memory_space=pl.ANY),
            scratch_shapes=[pltpu.VMEM((2, rows_per_step) + src.shape[1:], src.dtype),
                            pltpu.SemaphoreType.DMA((2,)), pltpu.SemaphoreType.DMA((2,))],
        ),
        out_shape=jax.ShapeDtypeStruct((n,) + src.shape[1:], src.dtype),
        compiler_params=_cp(("arbitrary",)),
        name="gather_rows",
    )(idx, src)


def _expert_changed(be_ref, i):
    return jnp.logical_or(i == 0, be_ref[i] != be_ref[jnp.maximum(i - 1, 0)])


def _expert_up_body(be_ref, x_ref, wg_ref, bg_ref, wu_ref, bu_ref, o_ref, wg_s, wu_s):
    @pl.when(_expert_changed(be_ref, pl.program_id(1)))
    def _():
        wg_s[...] = wg_ref[0].astype(BF16)
        wu_s[...] = wu_ref[0].astype(BF16)

    x = _unpack_rows(x_ref).astype(BF16)
    gate = jnp.dot(x, wg_s[...], preferred_element_type=F32) + bg_ref[0]
    up = jnp.dot(x, wu_s[...], preferred_element_type=F32) + bu_ref[0]
    gate = jnp.minimum(gate, SWIGLU_LIMIT)
    up = jnp.clip(up, -SWIGLU_LIMIT, SWIGLU_LIMIT)
    o_ref[...] = ((up + 1.0) * gate * _sigmoid(SWIGLU_ALPHA * gate)).astype(o_ref.dtype)


def _expert_down_body(be_ref, a_ref, wd_ref, bd_ref, o_ref, wd_s):
    @pl.when(_expert_changed(be_ref, pl.program_id(0)))
    def _():
        wd_s[...] = wd_ref[0].astype(BF16)

    _pack_rows(jnp.dot(a_ref[...], wd_s[...], preferred_element_type=F32) + bd_ref[0], o_ref)


def _experts(xs, block_e, w_gate, b_gate, w_up, b_up, w_down, b_down, *, tf=512):
    n_rows, ns, _ = xs.shape
    e, d, f = w_gate.shape
    nb = n_rows // EXPERT_BLOCK
    tf = _tile(f, tf, LANES)
    act = pl.pallas_call(
        _expert_up_body,
        grid_spec=pltpu.PrefetchScalarGridSpec(
            num_scalar_prefetch=1, grid=(f // tf, nb),
            in_specs=[pl.BlockSpec((EXPERT_BLOCK, ns, LANES), lambda j, i, be: (i, 0, 0)),
                      pl.BlockSpec((1, d, tf), lambda j, i, be: (be[i], 0, j)),
                      pl.BlockSpec((1, 1, tf), lambda j, i, be: (be[i], 0, j)),
                      pl.BlockSpec((1, d, tf), lambda j, i, be: (be[i], 0, j)),
                      pl.BlockSpec((1, 1, tf), lambda j, i, be: (be[i], 0, j))],
            out_specs=pl.BlockSpec((EXPERT_BLOCK, tf), lambda j, i, be: (i, j)),
            scratch_shapes=[pltpu.VMEM((d, tf), BF16), pltpu.VMEM((d, tf), BF16)]),
        out_shape=jax.ShapeDtypeStruct((n_rows, f), BF16),
        compiler_params=_cp(("arbitrary", "arbitrary")),
        name="expert_up",
    )(block_e, xs, w_gate, b_gate[:, None, :].astype(F32), w_up, b_up[:, None, :].astype(F32))
    return pl.pallas_call(
        _expert_down_body,
        grid_spec=pltpu.PrefetchScalarGridSpec(
            num_scalar_prefetch=1, grid=(nb,),
            in_specs=[pl.BlockSpec((EXPERT_BLOCK, f), lambda i, be: (i, 0)),
                      pl.BlockSpec((1, f, d), lambda i, be: (be[i], 0, 0)),
                      pl.BlockSpec((1, 1, d), lambda i, be: (be[i], 0, 0))],
            out_specs=pl.BlockSpec((EXPERT_BLOCK, ns, LANES), lambda i, be: (i, 0, 0)),
            scratch_shapes=[pltpu.VMEM((f, d), BF16)]),
        out_shape=jax.ShapeDtypeStruct((n_rows, ns, LANES), U32),
        compiler_params=_cp(("arbitrary",)),
        name="expert_down",
    )(block_e, act, w_down, b_down[:, None, :].astype(F32))


def _final_body(x1_ref, y_ref, w_ref, g2_ref, lng_ref, lnb_ref, o_ref, *, alpha):
    wts = w_ref[0]
    ff = None
    for kth in range(TOP_K):
        term = wts[:, kth:kth + 1] * _unpack_rows(y_ref.at[kth])
        ff = term if ff is None else ff + term
    o_ref[0] = (_ln(alpha * x1_ref[0] + g2_ref[0] * ff) * lng_ref[...] + lnb_ref[...]).astype(o_ref.dtype)


def _final(x1, ysel, top_w, mod3, ln_g, ln_b, out_dtype, *, alpha, tl=256):
    bsz, seq, d = x1.shape
    ns = ysel.shape[2]
    tl = min(tl, seq)
    nl = seq // tl
    full = lambda a: pl.BlockSpec(a.shape, lambda b, l: (0,) * a.ndim)
    lg, lb = ln_g[None].astype(F32), ln_b[None].astype(F32)
    return pl.pallas_call(
        functools.partial(_final_body, alpha=alpha),
        grid=(bsz, nl),
        in_specs=[pl.BlockSpec((1, tl, d), lambda b, l: (b, l, 0)),
                  pl.BlockSpec((TOP_K, tl, ns, LANES), lambda b, l: (0, b * nl + l, 0, 0)),
                  pl.BlockSpec((1, tl, LANES), lambda b, l: (b, l, 0)),
                  pl.BlockSpec((1, 1, d), lambda b, l: (b * 6 + 5, 0, 0)),
                  full(lg), full(lb)],
        out_specs=pl.BlockSpec((1, tl, d), lambda b, l: (b, l, 0)),
        out_shape=jax.ShapeDtypeStruct((bsz, seq, d), out_dtype),
        compiler_params=_cp(("parallel", "parallel")),
        name="combine_ln",
    )(x1, ysel, top_w, mod3, lg, lb)


def _route(top_idx, n_experts):
    n_tok = top_idx.shape[0]
    n_assign = n_tok * TOP_K
    flat_e = top_idx.reshape(n_assign)
    iota = jnp.arange(n_assign, dtype=I32)
    _, order = lax.sort_key_val(flat_e, iota)
    _, inv_order = lax.sort_key_val(order, iota)
    experts = jnp.arange(n_experts, dtype=I32)
    onehot = flat_e[:, None] == experts[None, :]
    counts = jnp.sum(onehot, axis=0, dtype=I32)
    padded = (counts + EXPERT_BLOCK - 1) // EXPERT_BLOCK * EXPERT_BLOCK
    start = jnp.cumsum(counts) - counts
    pad_end = jnp.cumsum(padded)
    pad_start = pad_end - padded
    shift = jnp.sum(jnp.where(onehot, (pad_start - start)[None, :], 0), axis=1, dtype=I32)
    pos = inv_order + shift
    pos_kmajor = pos.reshape(n_tok, TOP_K).T.reshape(n_assign)
    n_blocks = -(-(n_assign + n_experts * (EXPERT_BLOCK - 1)) // EXPERT_BLOCK)
    block_start = jnp.arange(n_blocks, dtype=I32) * EXPERT_BLOCK
    block_e = jnp.minimum(jnp.sum(pad_end[None, :] <= block_start[:, None], axis=1, dtype=I32),
                          n_experts - 1)
    src_start = jnp.clip(block_start - (pad_start - start)[block_e], 0, n_assign)
    tok_sorted = jnp.concatenate([order // TOP_K, jnp.zeros((EXPERT_BLOCK,), I32)])
    row_tok = jax.vmap(lambda s: lax.dynamic_slice(tok_sorted, (s,), (EXPERT_BLOCK,)))(src_start)
    return row_tok.reshape(n_blocks * EXPERT_BLOCK), block_e, pos_kmajor


def kernel(x, c, ada_w, ada_b, w_in, shift_mu, rwkv_w0, rwkv_w2, rwkv_a0, rwkv_a2, rwkv_g2,
           rwkv_k_k, rwkv_k_a, rwkv_r_k, rwkv_ln_g, rwkv_ln_b, rwkv_proj,
           s5_a_re, s5_a_im, s5_log_dt, s5_b_re, s5_b_im, s5_c_re, s5_c_im, s5_d,
           s5_glu_w1, s5_glu_w2, w_out, ln1_g, ln1_b, ln2_g, ln2_b, router_w, router_b,
           exp_w_gate, exp_b_gate, exp_w_up, exp_b_up, exp_w_down, exp_b_down):
    out_dtype = x.dtype
    bsz, seq, d = x.shape
    depth = ada_w.shape[0]
    alpha = (2 * depth) ** 0.25
    dr = rwkv_w0.shape[1]
    ds = s5_glu_w1.shape[1]
    shift_w = shift_mu.shape[1]
    n_tok = bsz * seq
    n_experts = router_w.shape[2]
    x = x.astype(F32)
    c_pad = jnp.zeros((8, d), F32).at[:bsz].set(c.astype(F32))

    for l in range(depth):
        mod = _matmul(c_pad, ada_w[l], ada_b[l][None].astype(F32), out_dtype=F32, tm=8, tn=512,
                      name="adaln", silu_a=True)[:bsz]
        mod3 = mod.reshape(bsz * 6, 1, d)

        h = _ln_mod(x, mod3, 1, 0).reshape(n_tok, d)
        wi = w_in[l].astype(BF16)
        proj = lambda lo, hi, tn, name: _matmul(h, wi[:, lo:hi], out_dtype=BF16, tm=1024, tn=tn,
                                                name=name)
        p_rkv = proj(0, 3 * dr, 512, "proj_rkv").reshape(bsz, seq, 3 * dr)
        p_lora = proj(3 * dr, shift_w, 256, "proj_lora").reshape(bsz, seq, shift_w - 3 * dr)
        p_u = proj(shift_w, shift_w + ds, 512, "proj_s5").reshape(bsz, seq, ds)
        p_gates = proj(shift_w + ds, shift_w + ds + 2 * d, 512, "proj_gates")

        r, k, v, a, lw, g = _rwkv_prep(p_rkv, p_lora, shift_mu[l].astype(F32), rwkv_w0[l].astype(F32),
                                       rwkv_w2[l], rwkv_a0[l].astype(F32), rwkv_a2[l], rwkv_g2[l])
        yr = _rwkv_recurrence(r, k, v, a, lw, g, rwkv_k_k[l], rwkv_k_a[l], rwkv_r_k[l],
                              rwkv_ln_g[l], rwkv_ln_b[l])
        z = _s5_branch(p_u, s5_a_re[l], s5_a_im[l], s5_log_dt[l], s5_b_re[l], s5_b_im[l],
                       s5_c_re[l], s5_c_im[l], s5_d[l])
        merged = _merge(yr.reshape(n_tok, dr), z.reshape(n_tok, ds), rwkv_proj[l].astype(BF16),
                        s5_glu_w1[l].astype(BF16), s5_glu_w2[l].astype(BF16), p_gates)
        mix = _matmul(merged, w_out[l].astype(BF16), out_dtype=F32, tm=1024, tn=512, name="w_out")

        x1, h2, top_idx, top_w = _resid_ln_router(x, mix.reshape(bsz, seq, d), mod3, ln1_g[l], ln1_b[l],
                                                  router_w[l], router_b[l], alpha=alpha)
        row_tok, block_e, pos_kmajor = _route(top_idx.reshape(n_tok, LANES)[:, :TOP_K], n_experts)
        xs = _gather_rows(h2.reshape((n_tok,) + h2.shape[2:]), row_tok)
        ys = _experts(xs, block_e, exp_w_gate[l], exp_b_gate[l], exp_w_up[l], exp_b_up[l],
                      exp_w_down[l], exp_b_down[l])
        ysel = _gather_rows(ys, pos_kmajor)
        ysel = ysel.reshape((TOP_K, n_tok) + ysel.shape[1:])
        x = _final(x1, ysel, top_w, mod3, ln2_g[l], ln2_b[l], F32, alpha=alpha)
    return x.astype(out_dtype)
```

```python
import functools

import jax
import jax.numpy as jnp
from jax import lax
from jax.experimental import pallas as pl
from jax.experimental.pallas import tpu as pltpu

F32 = jnp.float32
BF16 = jnp.bfloat16
I32 = jnp.int32
U32 = jnp.uint32

LN_EPS = 1e-5
RWKV_GN_EPS = 64e-5
TOP_K = 4
EXPERT_BLOCK = 256
SWIGLU_LIMIT = 7.0
SWIGLU_ALPHA = 1.702
RWKV_CHUNK = 64
S5_CHUNK = 64
LANES = 128
VMEM_LIMIT = 56 * 1024 * 1024
HI = lax.Precision.HIGHEST


def _cp(sem, vmem=VMEM_LIMIT):
    return pltpu.CompilerParams(dimension_semantics=sem, vmem_limit_bytes=vmem)


def _dot(a, b):
    return jnp.dot(a.astype(BF16), b.astype(BF16), preferred_element_type=F32)


def _dot_nt(a, b, precision=None):
    return lax.dot_general(a, b, (((1,), (1,)), ((), ())), preferred_element_type=F32,
                           precision=precision)


def _dot_tn(a, b):
    return lax.dot_general(a.astype(BF16), b.astype(BF16), (((0,), (0,)), ((), ())),
                           preferred_element_type=F32)


def _ln(x):
    mean = jnp.mean(x, axis=-1, keepdims=True)
    xc = x - mean
    var = jnp.mean(xc * xc, axis=-1, keepdims=True)
    return xc * lax.rsqrt(var + LN_EPS)


def _sigmoid(x):
    return 1.0 / (1.0 + jnp.exp(-x))


def _tile(n, pref, align):
    if n <= pref:
        return n
    t = pref - pref % align
    while n % t:
        t -= align
    return t


def _pack_geometry(d):
    return d // (2 * LANES), LANES


def _pack_rows(x, o_ref):
    rows, d = x.shape
    half = d // 2
    ns, lw = o_ref.shape[0] // rows, o_ref.shape[1]
    lo = lax.bitcast_convert_type(x[:, :half].astype(BF16).astype(F32), U32) >> 16
    hi = lax.bitcast_convert_type(x[:, half:].astype(BF16).astype(F32), U32) & jnp.uint32(0xFFFF0000)
    words = hi | lo
    for s in range(ns):
        o_ref[pl.ds(s, rows, stride=ns), :] = words[:, s * lw:(s + 1) * lw]


def _unpack_rows(x_ref, rows):
    ns = x_ref.shape[0] // rows
    los, his = [], []
    for s in range(ns):
        words = x_ref[pl.ds(s, rows, stride=ns), :]
        los.append(lax.bitcast_convert_type(words << 16, F32))
        his.append(lax.bitcast_convert_type(words & jnp.uint32(0xFFFF0000), F32))
    return jnp.concatenate(los + his, axis=-1)


def _mm_body(a_ref, b_ref, bias_ref, o_ref, *, silu_a):
    a = a_ref[...]
    if silu_a:
        a = a * _sigmoid(a)
    o_ref[...] = (_dot(a, b_ref[...]) + bias_ref[...]).astype(o_ref.dtype)


def _matmul(a, b, bias=None, *, out_dtype, tm, tn, name, silu_a=False):
    m, k = a.shape
    n = b.shape[1]
    tm, tn = _tile(m, tm, 8), _tile(n, tn, LANES)
    if bias is None:
        bias = jnp.zeros((1, n), F32)
    return pl.pallas_call(
        functools.partial(_mm_body, silu_a=silu_a),
        grid=(m // tm, n // tn),
        in_specs=[pl.BlockSpec((tm, k), lambda i, j: (i, 0)),
                  pl.BlockSpec((k, tn), lambda i, j: (0, j)),
                  pl.BlockSpec((1, tn), lambda i, j: (0, j))],
        out_specs=pl.BlockSpec((tm, tn), lambda i, j: (i, j)),
        out_shape=jax.ShapeDtypeStruct((m, n), out_dtype),
        compiler_params=_cp(("parallel", "parallel")),
        name=name,
    )(a, b, bias)


def _ln_mod_body(x_ref, sc_ref, sh_ref, o_ref):
    y = _ln(x_ref[0])
    o_ref[0] = (y * (1.0 + sc_ref[0]) + sh_ref[0]).astype(o_ref.dtype)


def _ln_mod(x, mod3, sc_idx, sh_idx, *, tl=256):
    bsz, seq, d = x.shape
    tl = min(tl, seq)
    return pl.pallas_call(
        _ln_mod_body,
        grid=(bsz, seq // tl),
        in_specs=[pl.BlockSpec((1, tl, d), lambda b, l: (b, l, 0)),
                  pl.BlockSpec((1, 1, d), lambda b, l: (b * 6 + sc_idx, 0, 0)),
                  pl.BlockSpec((1, 1, d), lambda b, l: (b * 6 + sh_idx, 0, 0))],
        out_specs=pl.BlockSpec((1, tl, d), lambda b, l: (b, l, 0)),
        out_shape=jax.ShapeDtypeStruct((bsz, seq, d), BF16),
        compiler_params=_cp(("parallel", "parallel")),
        name="ln_modulate",
    )(x, mod3, mod3)


def _token_shift(x, carry_ref, mu):
    rolled = pltpu.roll(x, 1, 0)
    row = lax.broadcasted_iota(I32, x.shape, 0)
    prev = jnp.where(row == 0, carry_ref[...], rolled)
    carry_ref[...] = x[x.shape[0] - 1:, :]
    return x + (prev - x) * mu


def _rwkv_prep_body(prkv_ref, plora_ref, mu_rkv_ref, mu_lora_ref, w0_ref, w2_ref, a0_ref, a2_ref,
                    g2_ref, r_ref, k_ref, v_ref, a_ref, lw_ref, g_ref, carry_rkv, carry_lora,
                    *, dr, w_lora, a_lora):
    @pl.when(pl.program_id(1) == 0)
    def _():
        carry_rkv[...] = jnp.zeros_like(carry_rkv)
        carry_lora[...] = jnp.zeros_like(carry_lora)

    xs = _token_shift(prkv_ref[0].astype(F32), carry_rkv, mu_rkv_ref[...])
    r_ref[0] = xs[:, :dr].astype(r_ref.dtype)
    k_ref[0] = xs[:, dr:2 * dr].astype(k_ref.dtype)
    v_ref[0] = xs[:, 2 * dr:].astype(v_ref.dtype)

    ls = _token_shift(plora_ref[0].astype(F32), carry_lora, mu_lora_ref[...])
    xw = ls[:, :w_lora]
    xa = ls[:, w_lora:w_lora + a_lora]
    xg = ls[:, w_lora + a_lora:]
    z = -(w0_ref[...] + _dot(jnp.tanh(xw), w2_ref[...]))
    softplus = jnp.maximum(z, 0.0) + jnp.log1p(jnp.exp(-jnp.abs(z)))
    lw_ref[0] = -jnp.exp(-softplus - 0.5)
    a_ref[0] = _sigmoid(a0_ref[...] + _dot(xa, a2_ref[...])).astype(a_ref.dtype)
    g_ref[0] = _dot(_sigmoid(xg), g2_ref[...]).astype(g_ref.dtype)


def _rwkv_prep(p_rkv, p_lora, mu, w0, w2, a0, a2, g2, *, tl=256):
    bsz, seq, dr3 = p_rkv.shape
    dr = dr3 // 3
    lora = p_lora.shape[-1]
    w_lora, a_lora = w2.shape[0], a2.shape[0]
    tl = min(tl, seq)
    tok = lambda w: pl.BlockSpec((1, tl, w), lambda b, l: (b, l, 0))
    full = lambda a: pl.BlockSpec(a.shape, lambda b, l: (0,) * a.ndim)
    mu_rkv, mu_lora = mu[None, :dr3], mu[None, dr3:]
    args = (p_rkv, p_lora, mu_rkv, mu_lora, w0[None], w2.astype(BF16), a0[None], a2.astype(BF16),
            g2.astype(BF16))
    out = jax.ShapeDtypeStruct((bsz, seq, dr), BF16)
    return pl.pallas_call(
        functools.partial(_rwkv_prep_body, dr=dr, w_lora=w_lora, a_lora=a_lora),
        grid=(bsz, seq // tl),
        in_specs=[tok(dr3), tok(lora)] + [full(a) for a in args[2:]],
        out_specs=[tok(dr)] * 6,
        out_shape=[out, out, out, out, jax.ShapeDtypeStruct((bsz, seq, dr), F32), out],
        scratch_shapes=[pltpu.VMEM((1, dr3), F32), pltpu.VMEM((1, lora), F32)],
        compiler_params=_cp(("parallel", "arbitrary")),
        name="rwkv_prep",
    )(*args)


def _split_bf16(x):
    hi = x.astype(BF16)
    return hi, (x - hi.astype(F32)).astype(BF16)


def _head_sum(x, bd_ref):
    c, w = x.shape
    gw = bd_ref.shape[0]
    ng = w // gw
    hi, lo = _split_bf16(x)
    groups = [p[:, j * gw:(j + 1) * gw] for p in (hi, lo) for j in range(ng)]
    res = jnp.dot(jnp.concatenate(groups, axis=0), bd_ref[...], preferred_element_type=F32)
    parts = [res[j * c:(j + 1) * c] + res[(ng + j) * c:(ng + j + 1) * c] for j in range(ng)]
    return jnp.concatenate(parts, axis=-1)


def _rwkv_rec_body(r_ref, k_ref, v_ref, a_ref, lw_ref, g_ref, kk_ref, ka_ref, rk_ref, lng_ref,
                   lnb_ref, bd_ref, y_ref, state, ybuf, *, heads, n):
    c = r_ref.shape[1]

    @pl.when(pl.program_id(2) == 0)
    def _():
        state[...] = jnp.zeros_like(state)

    row = lax.broadcasted_iota(I32, (c, c), 0)
    col = lax.broadcasted_iota(I32, (c, c), 1)
    strict = row > col
    eye = (row == col).astype(F32)
    tril_b = (row >= col).astype(BF16)
    row2 = lax.broadcasted_iota(I32, (c, 2 * c), 0)
    col2 = lax.broadcasted_iota(I32, (c, 2 * c), 1)
    strict_right = (col2 >= c) & (row2 > col2 - c)
    lower_pair = row2 >= jnp.where(col2 >= c, col2 - c, col2)

    r = r_ref[0].astype(F32)
    k = k_ref[0].astype(F32)
    v = v_ref[0].astype(F32)
    a = a_ref[0].astype(F32)
    lw = lw_ref[0]
    lw_hi, lw_lo = _split_bf16(lw)
    cl = (jnp.dot(tril_b, lw_hi, preferred_element_type=F32)
          + jnp.dot(tril_b, lw_lo, preferred_element_type=F32))
    w_inc = jnp.exp(cl)
    w_inv = jnp.exp(-cl)
    w_prev = jnp.exp(cl - lw)
    kk = k * kk_ref[...]
    kk = kk / jnp.maximum(jnp.sqrt(_head_sum(kk * kk, bd_ref)), 1e-12)
    k2 = k * (1.0 + (a - 1.0) * ka_ref[...])
    rt = (r * w_inc).astype(BF16)
    kkt = (kk * w_prev).astype(BF16)
    bt = (kk * a * w_inv).astype(BF16)
    kt = (k2 * w_inv).astype(BF16)
    vb = v.astype(BF16)
    bonus = _head_sum(r * k2 * rk_ref[...], bd_ref) * v

    hs = range(heads)
    sls = [slice(h * n, (h + 1) * n) for h in hs]
    lhs2 = [jnp.concatenate([kkt[:, sl], rt[:, sl]], axis=0) for sl in sls]
    rhs2 = [jnp.concatenate([bt[:, sl], kt[:, sl]], axis=0) for sl in sls]
    vh = [vb[:, sl] for sl in sls]
    s0 = [state[h] for h in hs]
    gram = [_dot_nt(lhs2[h], rhs2[h]) for h in hs]
    m1 = [_dot_nt(lhs2[h], s0[h].astype(BF16)) for h in hs]
    l_b = [jnp.where(strict, gram[h][:c, :c], 0.0) for h in hs]
    lk_only = [jnp.where(strict_right, gram[h][:c], 0.0).astype(BF16) for h in hs]
    p_bk = [jnp.where(lower_pair, gram[h][c:], 0.0).astype(BF16) for h in hs]
    rhs_u = [m1[h][:c] + jnp.dot(lk_only[h], jnp.concatenate([vh[h], vh[h]], axis=0),
                                 preferred_element_type=F32) for h in hs]

    t_inv = [eye - l_b[h] for h in hs]
    pw = [_dot(l_b[h], l_b[h]) for h in hs]
    span = 4
    while span < c:
        both = [_dot(jnp.concatenate([t_inv[h], pw[h]], axis=0), pw[h]) for h in hs]
        t_inv = [t_inv[h] + both[h][:c] for h in hs]
        pw = [both[h][c:] for h in hs]
        span *= 2
    t_inv = [t_inv[h] + _dot(t_inv[h], pw[h]) for h in hs]

    u = [-_dot(t_inv[h], rhs_u[h]) for h in hs]
    uv = [jnp.concatenate([u[h].astype(BF16), vh[h]], axis=0) for h in hs]
    y_h = [m1[h][c:] + jnp.dot(p_bk[h], uv[h], preferred_element_type=F32) for h in hs]
    s_new = [s0[h] + _dot_tn(uv[h], rhs2[h]) for h in hs]
    for h in hs:
        ybuf[:, sls[h]] = y_h[h]
        state[h] = s_new[h] * w_inc[c - 1:, sls[h]]

    y = ybuf[...]
    inv_n = 1.0 / n
    yc = y - _head_sum(y, bd_ref) * inv_n
    yv = _head_sum(yc * yc, bd_ref) * inv_n
    yn = yc * lax.rsqrt(yv + RWKV_GN_EPS) * lng_ref[...] + lnb_ref[...]
    y_ref[0] = ((yn + bonus) * g_ref[0].astype(F32)).astype(y_ref.dtype)


def _rwkv_recurrence(r, k, v, a, lw, g, k_k, k_a, r_k, ln_g, ln_b, *, heads_per_step=16):
    bsz, seq, dr = r.shape
    n = k_k.shape[-1]
    hb = min(heads_per_step, dr // n)
    wblk = hb * n
    gw = min(2 * LANES, wblk)
    assert wblk % LANES == 0 and dr % wblk == 0 and seq % RWKV_CHUNK == 0 and wblk % gw == 0
    c = RWKV_CHUNK
    tok = pl.BlockSpec((1, c, wblk), lambda b, h, t: (b, t, h))
    par = pl.BlockSpec((1, wblk), lambda b, h, t: (0, h))
    flat = lambda p: p.reshape(1, dr).astype(F32)
    gi = lax.broadcasted_iota(I32, (gw, gw), 0) // n
    gj = lax.broadcasted_iota(I32, (gw, gw), 1) // n
    block_diag = (gi == gj).astype(BF16)
    return pl.pallas_call(
        functools.partial(_rwkv_rec_body, heads=hb, n=n),
        grid=(bsz, dr // wblk, seq // c),
        in_specs=[tok] * 6 + [par] * 5 + [pl.BlockSpec((gw, gw), lambda b, h, t: (0, 0))],
        out_specs=tok,
        out_shape=jax.ShapeDtypeStruct((bsz, seq, dr), BF16),
        scratch_shapes=[pltpu.VMEM((hb, n, n), F32), pltpu.VMEM((c, wblk), F32)],
        compiler_params=_cp(("parallel", "parallel", "arbitrary")),
        name="rwkv_recurrence",
    )(r, k, v, a, lw, g, flat(k_k), flat(k_a), flat(r_k), flat(ln_g), flat(ln_b), block_diag)


def _s5_param_body(are_ref, aim_ref, ldt_ref, bre_ref, bim_ref, cre_ref, cim_ref, d_ref,
                   t_ref, bmat_ref, cmat_ref, q_ref, *, lc, n_levels):
    a_re, a_im = are_ref[0], aim_ref[0]
    dt = jnp.exp(ldt_ref[0])
    lam = a_re * dt
    ang = a_im * dt
    mag = jnp.exp(lam)
    ab_re, ab_im = mag * jnp.cos(ang), mag * jnp.sin(ang)
    den = a_re * a_re + a_im * a_im
    f_re = ((ab_re - 1.0) * a_re + ab_im * a_im) / den
    f_im = (ab_im * a_re - (ab_re - 1.0) * a_im) / den
    b_re, b_im = bre_ref[0], bim_ref[0]
    bb_re = f_re * b_re - f_im * b_im
    bb_im = f_re * b_im + f_im * b_re
    c_re, c_im = cre_ref[0], cim_ref[0]
    nh, p2 = b_re.shape
    lo = lax.broadcasted_iota(I32, (1, p2), 1) < (p2 // 2)

    def power(tf):
        m = jnp.exp(tf * lam)
        return m * jnp.cos(tf * ang), m * jnp.sin(tf * ang)

    def cmul3(xr, xi, pr, pi):
        rr = xr[None] * pr[:, None, :] - xi[None] * pi[:, None, :]
        ri = xr[None] * pi[:, None, :] + xi[None] * pr[:, None, :]
        return rr.reshape(lc * nh, p2), ri.reshape(lc * nh, p2)

    tcol = lax.broadcasted_iota(I32, (lc, 1), 0).astype(F32)
    nr, ni = power(-tcol)
    pr, pi = power(tcol)
    bm_re, bm_im = cmul3(bb_re, bb_im, nr, ni)
    cm_re, cm_im = cmul3(c_re, c_im, pr, pi)
    x = jnp.where(lo, bm_re, -bm_im)
    yc = jnp.where(lo, cm_re, cm_im)
    x_hi, x_lo = _split_bf16(x)
    y_hi, y_lo = _split_bf16(yc)
    t = _dot_nt(x_hi, y_hi) + _dot_nt(x_hi, y_lo) + _dot_nt(x_lo, y_hi)
    m = lc * nh
    ri = lax.broadcasted_iota(I32, (m, m), 0)
    ci = lax.broadcasted_iota(I32, (m, m), 1)
    t = jnp.where(ci // nh >= ri // nh, t, 0.0)
    t = jnp.where(ri == ci, t + d_ref[0], t)
    t_ref[0] = t.astype(t_ref.dtype)

    er, ei = power(jnp.full((1, 1), lc - 1.0, F32))
    bmat_ref[0] = jnp.where(lo, bm_re * er - bm_im * ei, bm_re * ei + bm_im * er).astype(bmat_ref.dtype)
    cmat_ref[0] = jnp.where(lo, cm_re * ab_re - cm_im * ab_im,
                            -(cm_re * ab_im + cm_im * ab_re)).astype(cmat_ref.dtype)
    for i in range(n_levels):
        qr, qi = power(jnp.full((1, 1), float(lc * 2 ** i), F32))
        q_ref[0, i:i + 1, :] = qr
        q_ref[0, n_levels + i:n_levels + i + 1, :] = jnp.where(lo, -qi, qi)


def _s5_conv_body(u_ref, t_ref, bmat_ref, cmat_ref, q_ref, z_ref, *, nc, n_levels):
    u = u_ref[0]
    zloc = jnp.dot(u, bmat_ref[0], preferred_element_type=F32)
    rows, p2 = zloc.shape
    n_in_seq = lax.broadcasted_iota(I32, (rows, p2), 0) % nc
    s = jnp.where(n_in_seq >= 1, pltpu.roll(zloc, 1, 0), 0.0)
    for i in range(n_levels):
        step = 2 ** i
        sh = jnp.where(n_in_seq >= step, pltpu.roll(s, step, 0), 0.0)
        sw = pltpu.roll(sh, p2 // 2, 1)
        s = s + q_ref[0, i:i + 1, :] * sh + q_ref[0, n_levels + i:n_levels + i + 1, :] * sw
    y = jnp.dot(u, t_ref[0], preferred_element_type=F32) + _dot_nt(s.astype(BF16), cmat_ref[0])
    z_ref[0] = jax.nn.gelu(y).astype(z_ref.dtype)


def _s5_branch(p_u, a_re, a_im, log_dt, b_re, b_im, c_re, c_im, d):
    bsz, seq, ds = p_u.shape
    g, p = a_re.shape
    nh = ds // g
    lc = min(S5_CHUNK, seq)
    nc = seq // lc
    n_levels = max(1, (nc - 1).bit_length())
    rows = bsz * nc
    m = lc * nh
    tile2 = lambda a: jnp.concatenate([a, a], axis=-1).astype(F32)
    args = (tile2(a_re)[:, None, :], tile2(a_im)[:, None, :], log_dt.reshape(g, 1, 1).astype(F32),
            tile2(jnp.swapaxes(b_re, 1, 2)), tile2(jnp.swapaxes(b_im, 1, 2)),
            tile2(c_re), tile2(c_im), jnp.tile(d.astype(F32), (1, lc))[:, None, :])
    gspec = lambda a: pl.BlockSpec((1,) + a.shape[1:], lambda i: (i, 0, 0))
    qrows = 2 * n_levels
    t_mat, bmat, cmat, q = pl.pallas_call(
        functools.partial(_s5_param_body, lc=lc, n_levels=n_levels),
        grid=(g,),
        in_specs=[gspec(a) for a in args],
        out_specs=[pl.BlockSpec((1, m, m), lambda i: (i, 0, 0)),
                   pl.BlockSpec((1, m, 2 * p), lambda i: (i, 0, 0)),
                   pl.BlockSpec((1, m, 2 * p), lambda i: (i, 0, 0)),
                   pl.BlockSpec((1, qrows, 2 * p), lambda i: (i, 0, 0))],
        out_shape=[jax.ShapeDtypeStruct((g, m, m), BF16),
                   jax.ShapeDtypeStruct((g, m, 2 * p), BF16),
                   jax.ShapeDtypeStruct((g, m, 2 * p), BF16),
                   jax.ShapeDtypeStruct((g, qrows, 2 * p), F32)],
        compiler_params=_cp(("parallel",)),
        name="s5_params",
    )(*args)

    u = p_u.reshape(bsz, nc, lc, g, nh).transpose(3, 0, 1, 2, 4).reshape(g, rows, m)
    z = pl.pallas_call(
        functools.partial(_s5_conv_body, nc=nc, n_levels=n_levels),
        grid=(g,),
        in_specs=[gspec(u), gspec(t_mat), gspec(bmat), gspec(cmat), gspec(q)],
        out_specs=pl.BlockSpec((1, rows, m), lambda i: (i, 0, 0)),
        out_shape=jax.ShapeDtypeStruct((g, rows, m), BF16),
        compiler_params=_cp(("parallel",)),
        name="s5_conv",
    )(u, t_mat, bmat, cmat, q)
    return z.reshape(g, bsz, nc, lc, nh).transpose(1, 2, 3, 0, 4).reshape(bsz, seq, ds)


def _merge_body(yr_ref, z_ref, wp_ref, w1_ref, w2_ref, gr_ref, gs_ref, o_ref):
    y_rwkv = jnp.dot(yr_ref[...], wp_ref[...], preferred_element_type=F32)
    z = z_ref[...]
    y_s5 = (jnp.dot(z, w1_ref[...], preferred_element_type=F32)
            * _sigmoid(jnp.dot(z, w2_ref[...], preferred_element_type=F32)))
    o_ref[...] = (_sigmoid(gr_ref[...].astype(F32)) * y_rwkv
                  + _sigmoid(gs_ref[...].astype(F32)) * y_s5).astype(o_ref.dtype)


def _merge(yr, z, wp, w1, w2, p_gates, *, tm=512, tn=512):
    t, kr = yr.shape
    ks = z.shape[1]
    d = wp.shape[1]
    tm, tn = min(tm, t), min(tn, d)
    nj = d // tn
    return pl.pallas_call(
        _merge_body,
        grid=(t // tm, nj),
        in_specs=[pl.BlockSpec((tm, kr), lambda i, j: (i, 0)),
                  pl.BlockSpec((tm, ks), lambda i, j: (i, 0)),
                  pl.BlockSpec((kr, tn), lambda i, j: (0, j)),
                  pl.BlockSpec((ks, tn), lambda i, j: (0, j)),
                  pl.BlockSpec((ks, tn), lambda i, j: (0, j)),
                  pl.BlockSpec((tm, tn), lambda i, j: (i, j)),
                  pl.BlockSpec((tm, tn), lambda i, j: (i, j + nj))],
        out_specs=pl.BlockSpec((tm, tn), lambda i, j: (i, j)),
        out_shape=jax.ShapeDtypeStruct((t, d), BF16),
        compiler_params=_cp(("parallel", "parallel")),
        name="merge_branches",
    )(yr, z, wp, w1, w2, p_gates, p_gates)


def _router_body(x_ref, mix_ref, g1_ref, lng_ref, lnb_ref, sc_ref, sh_ref, rw_ref, rb_ref,
                 x1_ref, h2_ref, idx_ref, w_ref, *, alpha):
    x1 = _ln(alpha * x_ref[0] + g1_ref[0] * mix_ref[0].astype(F32)) * lng_ref[...] + lnb_ref[...]
    x1_ref[0] = x1
    h2 = _ln(x1) * (1.0 + sc_ref[0]) + sh_ref[0]
    _pack_rows(h2, h2_ref.at[0])
    logits = jnp.dot(h2, rw_ref[...], preferred_element_type=F32, precision=HI) + rb_ref[...]
    lane = lax.broadcasted_iota(I32, logits.shape, 1)
    lane_f = lane.astype(F32)
    idx_out = jnp.zeros(logits.shape, I32)
    val_out = jnp.zeros(logits.shape, F32)
    vals = []
    for kth in range(TOP_K):
        mx = jnp.max(logits, axis=-1, keepdims=True)
        sel = jnp.min(jnp.where(logits == mx, lane_f, float(LANES)), axis=-1, keepdims=True).astype(I32)
        vals.append(mx)
        idx_out = jnp.where(lane == kth, sel, idx_out)
        logits = jnp.where(lane == sel, -jnp.inf, logits)
    exps = [jnp.exp(vv - vals[0]) for vv in vals]
    tot = exps[0]
    for e in exps[1:]:
        tot = tot + e
    for kth in range(TOP_K):
        val_out = jnp.where(lane == kth, exps[kth] / tot, val_out)
    idx_ref[0] = idx_out
    w_ref[0] = val_out


def _resid_ln_router(x, mix, mod3, ln_g, ln_b, router_w, router_b, *, alpha, tl=256):
    bsz, seq, d = x.shape
    e = router_w.shape[1]
    tl = min(tl, seq)
    ns, lw = _pack_geometry(d)
    rw = jnp.zeros((d, LANES), F32).at[:, :e].set(router_w.astype(F32))
    rb = jnp.full((1, LANES), -jnp.inf, F32).at[0, :e].set(router_b.astype(F32))
    tok = lambda w: pl.BlockSpec((1, tl, w), lambda b, l: (b, l, 0))
    modspec = lambda j: pl.BlockSpec((1, 1, d), lambda b, l: (b * 6 + j, 0, 0))
    full = lambda a: pl.BlockSpec(a.shape, lambda b, l: (0,) * a.ndim)
    args = (x, mix, mod3, ln_g[None].astype(F32), ln_b[None].astype(F32), mod3, mod3, rw, rb)
    return pl.pallas_call(
        functools.partial(_router_body, alpha=alpha),
        grid=(bsz, seq // tl),
        in_specs=[tok(d), tok(d), modspec(2), full(args[3]), full(args[4]), modspec(4), modspec(3),
                  full(rw), full(rb)],
        out_specs=[tok(d), pl.BlockSpec((1, tl * ns, lw), lambda b, l: (b, l, 0)),
                   tok(LANES), tok(LANES)],
        out_shape=[jax.ShapeDtypeStruct((bsz, seq, d), F32),
                   jax.ShapeDtypeStruct((bsz, seq * ns, lw), U32),
                   jax.ShapeDtypeStruct((bsz, seq, LANES), I32),
                   jax.ShapeDtypeStruct((bsz, seq, LANES), F32)],
        compiler_params=_cp(("parallel", "parallel")),
        name="resid_ln_router",
    )(*args)


def _gather_body(idx_ref, src_ref, o_ref, buf, gather_sem, write_sem, *, rows_per_step, n_steps):
    def row_copy(step, slot, r):
        return pltpu.make_async_copy(src_ref.at[idx_ref[step * rows_per_step + r]],
                                     buf.at[slot, r], gather_sem.at[slot])

    def start_gather(step, slot):
        def body(r, carry):
            row_copy(step, slot, r).start()
            return carry
        lax.fori_loop(0, rows_per_step, body, 0, unroll=8)

    def wait_gather(step, slot):
        def body(r, carry):
            row_copy(step, slot, r).wait()
            return carry
        lax.fori_loop(0, rows_per_step, body, 0, unroll=8)

    def write_back(step, slot):
        return pltpu.make_async_copy(buf.at[slot], o_ref.at[pl.ds(step * rows_per_step, rows_per_step)],
                                     write_sem.at[slot])

    i = pl.program_id(0)
    slot = i % 2
    other = 1 - slot

    @pl.when(i == 0)
    def _():
        start_gather(0, 0)

    @pl.when(i >= 1)
    def _():
        write_back(i - 1, other).wait()

    @pl.when(i + 1 < n_steps)
    def _():
        start_gather(i + 1, other)

    wait_gather(i, slot)
    write_back(i, slot).start()

    @pl.when(i == n_steps - 1)
    def _():
        write_back(i, slot).wait()


def _gather_rows(src, idx, *, rows_per_step=256):
    n = idx.shape[0]
    rows_per_step = min(rows_per_step, n)
    assert n % rows_per_step == 0 and src.ndim == 3 and src.shape[2] % LANES == 0
    return pl.pallas_call(
        functools.partial(_gather_body, rows_per_step=rows_per_step, n_steps=n // rows_per_step),
        grid_spec=pltpu.PrefetchScalarGridSpec(
            num_scalar_prefetch=1,
            grid=(n // rows_per_step,),
            in_specs=[pl.BlockSpec(memory_space=pl.ANY)],
            out_specs=pl.BlockSpec(memory_space=pl.ANY),
            scratch_shapes=[pltpu.VMEM((2, rows_per_step) + src.shape[1:], src.dtype),
                            pltpu.SemaphoreType.DMA((2,)), pltpu.SemaphoreType.DMA((2,))],
        ),
        out_shape=jax.ShapeDtypeStruct((n,) + src.shape[1:], src.dtype),
        compiler_params=_cp(("arbitrary",)),
        name="gather_rows",
    )(idx, src)


def _expert_changed(be_ref, i):
    return jnp.logical_or(i == 0, be_ref[i] != be_ref[jnp.maximum(i - 1, 0)])


def _expert_up_body(be_ref, x_ref, wg_ref, bg_ref, wu_ref, bu_ref, o_ref, wg_s, wu_s):
    @pl.when(_expert_changed(be_ref, pl.program_id(1)))
    def _():
        wg_s[...] = wg_ref[0].astype(BF16)
        wu_s[...] = wu_ref[0].astype(BF16)

    x = _unpack_rows(x_ref, o_ref.shape[0]).astype(BF16)
    gate = jnp.dot(x, wg_s[...], preferred_element_type=F32) + bg_ref[0]
    up = jnp.dot(x, wu_s[...], preferred_element_type=F32) + bu_ref[0]
    gate = jnp.minimum(gate, SWIGLU_LIMIT)
    up = jnp.clip(up, -SWIGLU_LIMIT, SWIGLU_LIMIT)
    o_ref[...] = ((up + 1.0) * gate * _sigmoid(SWIGLU_ALPHA * gate)).astype(o_ref.dtype)


def _expert_down_body(be_ref, a_ref, wd_ref, bd_ref, o_ref, wd_s):
    @pl.when(_expert_changed(be_ref, pl.program_id(0)))
    def _():
        wd_s[...] = wd_ref[0].astype(BF16)

    _pack_rows(jnp.dot(a_ref[...], wd_s[...], preferred_element_type=F32) + bd_ref[0], o_ref)


def _experts(xs, block_e, w_gate, b_gate, w_up, b_up, w_down, b_down, *, tf=512):
    e, d, f = w_gate.shape
    ns, lw = _pack_geometry(d)
    n_rows = xs.shape[0] // ns
    nb = n_rows // EXPERT_BLOCK
    tf = _tile(f, tf, LANES)
    packed_rows = lambda imap: pl.BlockSpec((EXPERT_BLOCK * ns, lw), imap)
    act = pl.pallas_call(
        _expert_up_body,
        grid_spec=pltpu.PrefetchScalarGridSpec(
            num_scalar_prefetch=1, grid=(f // tf, nb),
            in_specs=[packed_rows(lambda j, i, be: (i, 0)),
                      pl.BlockSpec((1, d, tf), lambda j, i, be: (be[i], 0, j)),
                      pl.BlockSpec((1, 1, tf), lambda j, i, be: (be[i], 0, j)),
                      pl.BlockSpec((1, d, tf), lambda j, i, be: (be[i], 0, j)),
                      pl.BlockSpec((1, 1, tf), lambda j, i, be: (be[i], 0, j))],
            out_specs=pl.BlockSpec((EXPERT_BLOCK, tf), lambda j, i, be: (i, j)),
            scratch_shapes=[pltpu.VMEM((d, tf), BF16), pltpu.VMEM((d, tf), BF16)]),
        out_shape=jax.ShapeDtypeStruct((n_rows, f), BF16),
        compiler_params=_cp(("arbitrary", "arbitrary")),
        name="expert_up",
    )(block_e, xs, w_gate, b_gate[:, None, :].astype(F32), w_up, b_up[:, None, :].astype(F32))
    return pl.pallas_call(
        _expert_down_body,
        grid_spec=pltpu.PrefetchScalarGridSpec(
            num_scalar_prefetch=1, grid=(nb,),
            in_specs=[pl.BlockSpec((EXPERT_BLOCK, f), lambda i, be: (i, 0)),
                      pl.BlockSpec((1, f, d), lambda i, be: (be[i], 0, 0)),
                      pl.BlockSpec((1, 1, d), lambda i, be: (be[i], 0, 0))],
            out_specs=packed_rows(lambda i, be: (i, 0)),
            scratch_shapes=[pltpu.VMEM((f, d), BF16)]),
        out_shape=jax.ShapeDtypeStruct((n_rows * ns, lw), U32),
        compiler_params=_cp(("arbitrary",)),
        name="expert_down",
    )(block_e, act, w_down, b_down[:, None, :].astype(F32))


def _final_body(x1_ref, y_ref, w_ref, g2_ref, lng_ref, lnb_ref, o_ref, *, alpha):
    wts = w_ref[0]
    ff = None
    for kth in range(TOP_K):
        term = wts[:, kth:kth + 1] * _unpack_rows(y_ref.at[kth], wts.shape[0])
        ff = term if ff is None else ff + term
    o_ref[0] = (_ln(alpha * x1_ref[0] + g2_ref[0] * ff) * lng_ref[...] + lnb_ref[...]).astype(o_ref.dtype)


def _final(x1, ysel, top_w, mod3, ln_g, ln_b, out_dtype, *, alpha, tl=256):
    bsz, seq, d = x1.shape
    ns, lw = _pack_geometry(d)
    tl = min(tl, seq)
    nl = seq // tl
    full = lambda a: pl.BlockSpec(a.shape, lambda b, l: (0,) * a.ndim)
    lg, lb = ln_g[None].astype(F32), ln_b[None].astype(F32)
    return pl.pallas_call(
        functools.partial(_final_body, alpha=alpha),
        grid=(bsz, nl),
        in_specs=[pl.BlockSpec((1, tl, d), lambda b, l: (b, l, 0)),
                  pl.BlockSpec((TOP_K, tl * ns, lw), lambda b, l: (0, b * nl + l, 0)),
                  pl.BlockSpec((1, tl, LANES), lambda b, l: (b, l, 0)),
                  pl.BlockSpec((1, 1, d), lambda b, l: (b * 6 + 5, 0, 0)),
                  full(lg), full(lb)],
        out_specs=pl.BlockSpec((1, tl, d), lambda b, l: (b, l, 0)),
        out_shape=jax.ShapeDtypeStruct((bsz, seq, d), out_dtype),
        compiler_params=_cp(("parallel", "parallel")),
        name="combine_ln",
    )(x1, ysel, top_w, mod3, lg, lb)


def _route(top_idx, n_experts):
    n_tok = top_idx.shape[0]
    n_assign = n_tok * TOP_K
    flat_e = top_idx.reshape(n_assign)
    iota = jnp.arange(n_assign, dtype=I32)
    _, order = lax.sort_key_val(flat_e, iota)
    _, inv_order = lax.sort_key_val(order, iota)
    experts = jnp.arange(n_experts, dtype=I32)
    onehot = flat_e[:, None] == experts[None, :]
    counts = jnp.sum(onehot, axis=0, dtype=I32)
    padded = (counts + EXPERT_BLOCK - 1) // EXPERT_BLOCK * EXPERT_BLOCK
    start = jnp.cumsum(counts) - counts
    pad_end = jnp.cumsum(padded)
    pad_start = pad_end - padded
    shift = jnp.sum(jnp.where(onehot, (pad_start - start)[None, :], 0), axis=1, dtype=I32)
    pos = inv_order + shift
    pos_kmajor = pos.reshape(n_tok, TOP_K).T.reshape(n_assign)
    n_blocks = -(-(n_assign + n_experts * (EXPERT_BLOCK - 1)) // EXPERT_BLOCK)
    block_start = jnp.arange(n_blocks, dtype=I32) * EXPERT_BLOCK
    block_e = jnp.minimum(jnp.sum(pad_end[None, :] <= block_start[:, None], axis=1, dtype=I32),
                          n_experts - 1)
    src_start = jnp.clip(block_start - (pad_start - start)[block_e], 0, n_assign)
    tok_sorted = jnp.concatenate([order // TOP_K, jnp.zeros((EXPERT_BLOCK,), I32)])
    row_src = src_start[:, None] + jnp.arange(EXPERT_BLOCK, dtype=I32)[None, :]
    row_tok = jnp.take(tok_sorted, row_src.reshape(n_blocks * EXPERT_BLOCK))
    return row_tok, block_e, pos_kmajor


def kernel(x, c, ada_w, ada_b, w_in, shift_mu, rwkv_w0, rwkv_w2, rwkv_a0, rwkv_a2, rwkv_g2,
           rwkv_k_k, rwkv_k_a, rwkv_r_k, rwkv_ln_g, rwkv_ln_b, rwkv_proj,
           s5_a_re, s5_a_im, s5_log_dt, s5_b_re, s5_b_im, s5_c_re, s5_c_im, s5_d,
           s5_glu_w1, s5_glu_w2, w_out, ln1_g, ln1_b, ln2_g, ln2_b, router_w, router_b,
           exp_w_gate, exp_b_gate, exp_w_up, exp_b_up, exp_w_down, exp_b_down):
    out_dtype = x.dtype
    bsz, seq, d = x.shape
    depth = ada_w.shape[0]
    alpha = (2 * depth) ** 0.25
    dr = rwkv_w0.shape[1]
    ds = s5_glu_w1.shape[1]
    shift_w = shift_mu.shape[1]
    n_tok = bsz * seq
    n_experts = router_w.shape[2]
    x = x.astype(F32)
    c_pad = jnp.zeros((8, d), F32).at[:bsz].set(c.astype(F32))

    for l in range(depth):
        mod = _matmul(c_pad, ada_w[l], ada_b[l][None].astype(F32), out_dtype=F32, tm=8, tn=512,
                      name="adaln", silu_a=True)[:bsz]
        mod3 = mod.reshape(bsz * 6, 1, d)

        h = _ln_mod(x, mod3, 1, 0).reshape(n_tok, d)
        wi = w_in[l].astype(BF16)
        proj = lambda lo, hi, tn, name: _matmul(h, wi[:, lo:hi], out_dtype=BF16, tm=1024, tn=tn,
                                                name=name)
        p_rkv = proj(0, 3 * dr, 512, "proj_rkv").reshape(bsz, seq, 3 * dr)
        p_lora = proj(3 * dr, shift_w, 256, "proj_lora").reshape(bsz, seq, shift_w - 3 * dr)
        p_u = proj(shift_w, shift_w + ds, 512, "proj_s5").reshape(bsz, seq, ds)
        p_gates = proj(shift_w + ds, shift_w + ds + 2 * d, 512, "proj_gates")

        r, k, v, a, lw, g = _rwkv_prep(p_rkv, p_lora, shift_mu[l].astype(F32), rwkv_w0[l].astype(F32),
                                       rwkv_w2[l], rwkv_a0[l].astype(F32), rwkv_a2[l], rwkv_g2[l])
        yr = _rwkv_recurrence(r, k, v, a, lw, g, rwkv_k_k[l], rwkv_k_a[l], rwkv_r_k[l],
                              rwkv_ln_g[l], rwkv_ln_b[l])
        z = _s5_branch(p_u, s5_a_re[l], s5_a_im[l], s5_log_dt[l], s5_b_re[l], s5_b_im[l],
                       s5_c_re[l], s5_c_im[l], s5_d[l])
        merged = _merge(yr.reshape(n_tok, dr), z.reshape(n_tok, ds), rwkv_proj[l].astype(BF16),
                        s5_glu_w1[l].astype(BF16), s5_glu_w2[l].astype(BF16), p_gates)
        mix = _matmul(merged, w_out[l].astype(BF16), out_dtype=BF16, tm=1024, tn=512, name="w_out")

        x1, h2, top_idx, top_w = _resid_ln_router(x, mix.reshape(bsz, seq, d), mod3, ln1_g[l], ln1_b[l],
                                                  router_w[l], router_b[l], alpha=alpha)
        row_tok, block_e, pos_kmajor = _route(top_idx.reshape(n_tok, LANES)[:, :TOP_K], n_experts)
        ns, lw = _pack_geometry(d)
        xs = _gather_rows(h2.reshape(n_tok, ns, lw), row_tok)
        ys = _experts(xs.reshape(-1, lw), block_e, exp_w_gate[l], exp_b_gate[l], exp_w_up[l],
                      exp_b_up[l], exp_w_down[l], exp_b_down[l])
        ysel = _gather_rows(ys.reshape(-1, ns, lw), pos_kmajor).reshape(TOP_K, n_tok * ns, lw)
        x = _final(x1, ysel, top_w, mod3, ln2_g[l], ln2_b[l], F32, alpha=alpha)
    return x.astype(out_dtype)
```

```python
import functools

import jax
import jax.numpy as jnp
from jax import lax
from jax.experimental import pallas as pl
from jax.experimental.pallas import tpu as pltpu

F32 = jnp.float32
BF16 = jnp.bfloat16
I32 = jnp.int32
U32 = jnp.uint32

LN_EPS = 1e-5
RWKV_GN_EPS = 64e-5
TOP_K = 4
EXPERT_BLOCK = 256
SWIGLU_LIMIT = 7.0
SWIGLU_ALPHA = 1.702
RWKV_CHUNK = 64
S5_CHUNK = 64
LANES = 128
VMEM_LIMIT = 56 * 1024 * 1024
HI = lax.Precision.HIGHEST


def _cp(sem, vmem=VMEM_LIMIT):
    return pltpu.CompilerParams(dimension_semantics=sem, vmem_limit_bytes=vmem)


def _dot(a, b):
    return jnp.dot(a.astype(BF16), b.astype(BF16), preferred_element_type=F32)


def _dot_nt(a, b, precision=None):
    return lax.dot_general(a, b, (((1,), (1,)), ((), ())), preferred_element_type=F32,
                           precision=precision)


def _dot_tn(a, b):
    return lax.dot_general(a.astype(BF16), b.astype(BF16), (((0,), (0,)), ((), ())),
                           preferred_element_type=F32)


def _ln(x):
    mean = jnp.mean(x, axis=-1, keepdims=True)
    xc = x - mean
    var = jnp.mean(xc * xc, axis=-1, keepdims=True)
    return xc * lax.rsqrt(var + LN_EPS)


def _sigmoid(x):
    return 1.0 / (1.0 + jnp.exp(-x))


def _tile(n, pref, align):
    if n <= pref:
        return n
    t = pref - pref % align
    while n % t:
        t -= align
    return t


def _pack_geometry(d):
    return d // (2 * LANES), LANES


def _pack_rows(x, o_ref):
    rows, d = x.shape
    half = d // 2
    ns, lw = o_ref.shape[0] // rows, o_ref.shape[1]
    lo = lax.bitcast_convert_type(x[:, :half].astype(BF16).astype(F32), U32) >> 16
    hi = lax.bitcast_convert_type(x[:, half:].astype(BF16).astype(F32), U32) & jnp.uint32(0xFFFF0000)
    words = hi | lo
    for s in range(ns):
        o_ref[pl.ds(s, rows, stride=ns), :] = words[:, s * lw:(s + 1) * lw]


def _unpack_rows(x_ref, rows):
    ns = x_ref.shape[0] // rows
    los, his = [], []
    for s in range(ns):
        words = x_ref[pl.ds(s, rows, stride=ns), :]
        los.append(lax.bitcast_convert_type(words << 16, F32))
        his.append(lax.bitcast_convert_type(words & jnp.uint32(0xFFFF0000), F32))
    return jnp.concatenate(los + his, axis=-1)


def _mm_body(a_ref, b_ref, bias_ref, o_ref, *, silu_a):
    a = a_ref[...]
    if silu_a:
        a = a * _sigmoid(a)
    o_ref[...] = (_dot(a, b_ref[...]) + bias_ref[...]).astype(o_ref.dtype)


def _matmul(a, b, bias=None, *, out_dtype, tm, tn, name, silu_a=False):
    m, k = a.shape
    n = b.shape[1]
    tm, tn = _tile(m, tm, 8), _tile(n, tn, LANES)
    if bias is None:
        bias = jnp.zeros((1, n), F32)
    return pl.pallas_call(
        functools.partial(_mm_body, silu_a=silu_a),
        grid=(m // tm, n // tn),
        in_specs=[pl.BlockSpec((tm, k), lambda i, j: (i, 0)),
                  pl.BlockSpec((k, tn), lambda i, j: (0, j)),
                  pl.BlockSpec((1, tn), lambda i, j: (0, j))],
        out_specs=pl.BlockSpec((tm, tn), lambda i, j: (i, j)),
        out_shape=jax.ShapeDtypeStruct((m, n), out_dtype),
        compiler_params=_cp(("parallel", "parallel")),
        name=name,
    )(a, b, bias)


def _ln_mod_body(x_ref, sc_ref, sh_ref, o_ref):
    y = _ln(x_ref[0])
    o_ref[0] = (y * (1.0 + sc_ref[0]) + sh_ref[0]).astype(o_ref.dtype)


def _ln_mod(x, mod3, sc_idx, sh_idx, *, tl=256):
    bsz, seq, d = x.shape
    tl = min(tl, seq)
    return pl.pallas_call(
        _ln_mod_body,
        grid=(bsz, seq // tl),
        in_specs=[pl.BlockSpec((1, tl, d), lambda b, l: (b, l, 0)),
                  pl.BlockSpec((1, 1, d), lambda b, l: (b * 6 + sc_idx, 0, 0)),
                  pl.BlockSpec((1, 1, d), lambda b, l: (b * 6 + sh_idx, 0, 0))],
        out_specs=pl.BlockSpec((1, tl, d), lambda b, l: (b, l, 0)),
        out_shape=jax.ShapeDtypeStruct((bsz, seq, d), BF16),
        compiler_params=_cp(("parallel", "parallel")),
        name="ln_modulate",
    )(x, mod3, mod3)


def _token_shift(x, carry_ref, mu):
    rolled = pltpu.roll(x, 1, 0)
    row = lax.broadcasted_iota(I32, x.shape, 0)
    prev = jnp.where(row == 0, carry_ref[...], rolled)
    carry_ref[...] = x[x.shape[0] - 1:, :]
    return x + (prev - x) * mu


def _rwkv_prep_body(prkv_ref, plora_ref, mu_rkv_ref, mu_lora_ref, w0_ref, w2_ref, a0_ref, a2_ref,
                    g2_ref, r_ref, k_ref, v_ref, a_ref, lw_ref, g_ref, carry_rkv, carry_lora,
                    *, dr, w_lora, a_lora):
    @pl.when(pl.program_id(1) == 0)
    def _():
        carry_rkv[...] = jnp.zeros_like(carry_rkv)
        carry_lora[...] = jnp.zeros_like(carry_lora)

    xs = _token_shift(prkv_ref[0].astype(F32), carry_rkv, mu_rkv_ref[...])
    r_ref[0] = xs[:, :dr].astype(r_ref.dtype)
    k_ref[0] = xs[:, dr:2 * dr].astype(k_ref.dtype)
    v_ref[0] = xs[:, 2 * dr:].astype(v_ref.dtype)

    ls = _token_shift(plora_ref[0].astype(F32), carry_lora, mu_lora_ref[...])
    xw = ls[:, :w_lora]
    xa = ls[:, w_lora:w_lora + a_lora]
    xg = ls[:, w_lora + a_lora:]
    z = -(w0_ref[...] + _dot(jnp.tanh(xw), w2_ref[...]))
    softplus = jnp.maximum(z, 0.0) + jnp.log1p(jnp.exp(-jnp.abs(z)))
    lw_ref[0] = -jnp.exp(-softplus - 0.5)
    a_ref[0] = _sigmoid(a0_ref[...] + _dot(xa, a2_ref[...])).astype(a_ref.dtype)
    g_ref[0] = _dot(_sigmoid(xg), g2_ref[...]).astype(g_ref.dtype)


def _rwkv_prep(p_rkv, p_lora, mu, w0, w2, a0, a2, g2, *, tl=256):
    bsz, seq, dr3 = p_rkv.shape
    dr = dr3 // 3
    lora = p_lora.shape[-1]
    w_lora, a_lora = w2.shape[0], a2.shape[0]
    tl = min(tl, seq)
    tok = lambda w: pl.BlockSpec((1, tl, w), lambda b, l: (b, l, 0))
    full = lambda a: pl.BlockSpec(a.shape, lambda b, l: (0,) * a.ndim)
    mu_rkv, mu_lora = mu[None, :dr3], mu[None, dr3:]
    args = (p_rkv, p_lora, mu_rkv, mu_lora, w0[None], w2.astype(BF16), a0[None], a2.astype(BF16),
            g2.astype(BF16))
    out = jax.ShapeDtypeStruct((bsz, seq, dr), BF16)
    return pl.pallas_call(
        functools.partial(_rwkv_prep_body, dr=dr, w_lora=w_lora, a_lora=a_lora),
        grid=(bsz, seq // tl),
        in_specs=[tok(dr3), tok(lora)] + [full(a) for a in args[2:]],
        out_specs=[tok(dr)] * 6,
        out_shape=[out, out, out, out, jax.ShapeDtypeStruct((bsz, seq, dr), F32), out],
        scratch_shapes=[pltpu.VMEM((1, dr3), F32), pltpu.VMEM((1, lora), F32)],
        compiler_params=_cp(("parallel", "arbitrary")),
        name="rwkv_prep",
    )(*args)


def _split_bf16(x):
    hi = x.astype(BF16)
    return hi, (x - hi.astype(F32)).astype(BF16)


def _head_sum(x, bd_ref):
    c, w = x.shape
    gw = bd_ref.shape[0]
    ng = w // gw
    hi, lo = _split_bf16(x)
    groups = [p[:, j * gw:(j + 1) * gw] for p in (hi, lo) for j in range(ng)]
    res = jnp.dot(jnp.concatenate(groups, axis=0), bd_ref[...], preferred_element_type=F32)
    parts = [res[j * c:(j + 1) * c] + res[(ng + j) * c:(ng + j + 1) * c] for j in range(ng)]
    return jnp.concatenate(parts, axis=-1)


def _rwkv_rec_body(r_ref, k_ref, v_ref, a_ref, lw_ref, g_ref, kk_ref, ka_ref, rk_ref, lng_ref,
                   lnb_ref, bd_ref, y_ref, state, ybuf, *, heads, n):
    c = r_ref.shape[1]

    @pl.when(pl.program_id(2) == 0)
    def _():
        state[...] = jnp.zeros_like(state)

    row = lax.broadcasted_iota(I32, (c, c), 0)
    col = lax.broadcasted_iota(I32, (c, c), 1)
    strict = row > col
    eye = (row == col).astype(F32)
    tril_b = (row >= col).astype(BF16)
    row2 = lax.broadcasted_iota(I32, (c, 2 * c), 0)
    col2 = lax.broadcasted_iota(I32, (c, 2 * c), 1)
    strict_right = (col2 >= c) & (row2 > col2 - c)
    lower_pair = row2 >= jnp.where(col2 >= c, col2 - c, col2)

    r = r_ref[0].astype(F32)
    k = k_ref[0].astype(F32)
    v = v_ref[0].astype(F32)
    a = a_ref[0].astype(F32)
    lw = lw_ref[0]
    lw_hi, lw_lo = _split_bf16(lw)
    cl = (jnp.dot(tril_b, lw_hi, preferred_element_type=F32)
          + jnp.dot(tril_b, lw_lo, preferred_element_type=F32))
    w_inc = jnp.exp(cl)
    w_inv = jnp.exp(-cl)
    w_prev = jnp.exp(cl - lw)
    kk = k * kk_ref[...]
    kk = kk / jnp.maximum(jnp.sqrt(_head_sum(kk * kk, bd_ref)), 1e-12)
    k2 = k * (1.0 + (a - 1.0) * ka_ref[...])
    rt = (r * w_inc).astype(BF16)
    kkt = (kk * w_prev).astype(BF16)
    bt = (kk * a * w_inv).astype(BF16)
    kt = (k2 * w_inv).astype(BF16)
    vb = v.astype(BF16)
    bonus = _head_sum(r * k2 * rk_ref[...], bd_ref) * v

    hs = range(heads)
    sls = [slice(h * n, (h + 1) * n) for h in hs]
    lhs2 = [jnp.concatenate([kkt[:, sl], rt[:, sl]], axis=0) for sl in sls]
    rhs2 = [jnp.concatenate([bt[:, sl], kt[:, sl]], axis=0) for sl in sls]
    vh = [vb[:, sl] for sl in sls]
    s0 = [state[h] for h in hs]
    gram = [_dot_nt(lhs2[h], rhs2[h]) for h in hs]
    m1 = [_dot_nt(lhs2[h], s0[h].astype(BF16)) for h in hs]
    l_b = [jnp.where(strict, gram[h][:c, :c], 0.0) for h in hs]
    lk_only = [jnp.where(strict_right, gram[h][:c], 0.0).astype(BF16) for h in hs]
    p_bk = [jnp.where(lower_pair, gram[h][c:], 0.0).astype(BF16) for h in hs]
    rhs_u = [m1[h][:c] + jnp.dot(lk_only[h], jnp.concatenate([vh[h], vh[h]], axis=0),
                                 preferred_element_type=F32) for h in hs]

    t_inv = [eye - l_b[h] for h in hs]
    pw = [_dot(l_b[h], l_b[h]) for h in hs]
    span = 4
    while span < c:
        both = [_dot(jnp.concatenate([t_inv[h], pw[h]], axis=0), pw[h]) for h in hs]
        t_inv = [t_inv[h] + both[h][:c] for h in hs]
        pw = [both[h][c:] for h in hs]
        span *= 2
    t_inv = [t_inv[h] + _dot(t_inv[h], pw[h]) for h in hs]

    u = [-_dot(t_inv[h], rhs_u[h]) for h in hs]
    uv = [jnp.concatenate([u[h].astype(BF16), vh[h]], axis=0) for h in hs]
    y_h = [m1[h][c:] + jnp.dot(p_bk[h], uv[h], preferred_element_type=F32) for h in hs]
    s_new = [s0[h] + _dot_tn(uv[h], rhs2[h]) for h in hs]
    for h in hs:
        ybuf[:, sls[h]] = y_h[h]
        state[h] = s_new[h] * w_inc[c - 1:, sls[h]]

    y = ybuf[...]
    inv_n = 1.0 / n
    yc = y - _head_sum(y, bd_ref) * inv_n
    yv = _head_sum(yc * yc, bd_ref) * inv_n
    yn = yc * lax.rsqrt(yv + RWKV_GN_EPS) * lng_ref[...] + lnb_ref[...]
    y_ref[0] = ((yn + bonus) * g_ref[0].astype(F32)).astype(y_ref.dtype)


def _rwkv_recurrence(r, k, v, a, lw, g, k_k, k_a, r_k, ln_g, ln_b, *, heads_per_step=32):
    bsz, seq, dr = r.shape
    n = k_k.shape[-1]
    hb = min(heads_per_step, dr // n)
    wblk = hb * n
    gw = min(2 * LANES, wblk)
    assert wblk % LANES == 0 and dr % wblk == 0 and seq % RWKV_CHUNK == 0 and wblk % gw == 0
    c = RWKV_CHUNK
    tok = pl.BlockSpec((1, c, wblk), lambda b, h, t: (b, t, h))
    par = pl.BlockSpec((1, wblk), lambda b, h, t: (0, h))
    flat = lambda p: p.reshape(1, dr).astype(F32)
    gi = lax.broadcasted_iota(I32, (gw, gw), 0) // n
    gj = lax.broadcasted_iota(I32, (gw, gw), 1) // n
    block_diag = (gi == gj).astype(BF16)
    return pl.pallas_call(
        functools.partial(_rwkv_rec_body, heads=hb, n=n),
        grid=(bsz, dr // wblk, seq // c),
        in_specs=[tok] * 6 + [par] * 5 + [pl.BlockSpec((gw, gw), lambda b, h, t: (0, 0))],
        out_specs=tok,
        out_shape=jax.ShapeDtypeStruct((bsz, seq, dr), BF16),
        scratch_shapes=[pltpu.VMEM((hb, n, n), F32), pltpu.VMEM((c, wblk), F32)],
        compiler_params=_cp(("parallel", "parallel", "arbitrary")),
        name="rwkv_recurrence",
    )(r, k, v, a, lw, g, flat(k_k), flat(k_a), flat(r_k), flat(ln_g), flat(ln_b), block_diag)


def _s5_param_body(are_ref, aim_ref, ldt_ref, bre_ref, bim_ref, cre_ref, cim_ref, d_ref,
                   t_ref, bmat_ref, cmat_ref, q_ref, *, lc, n_levels):
    a_re, a_im = are_ref[0], aim_ref[0]
    dt = jnp.exp(ldt_ref[0])
    lam = a_re * dt
    ang = a_im * dt
    mag = jnp.exp(lam)
    ab_re, ab_im = mag * jnp.cos(ang), mag * jnp.sin(ang)
    den = a_re * a_re + a_im * a_im
    f_re = ((ab_re - 1.0) * a_re + ab_im * a_im) / den
    f_im = (ab_im * a_re - (ab_re - 1.0) * a_im) / den
    b_re, b_im = bre_ref[0], bim_ref[0]
    bb_re = f_re * b_re - f_im * b_im
    bb_im = f_re * b_im + f_im * b_re
    c_re, c_im = cre_ref[0], cim_ref[0]
    nh, p2 = b_re.shape
    lo = lax.broadcasted_iota(I32, (1, p2), 1) < (p2 // 2)

    def power(tf):
        m = jnp.exp(tf * lam)
        return m * jnp.cos(tf * ang), m * jnp.sin(tf * ang)

    def cmul3(xr, xi, pr, pi):
        rr = xr[None] * pr[:, None, :] - xi[None] * pi[:, None, :]
        ri = xr[None] * pi[:, None, :] + xi[None] * pr[:, None, :]
        return rr.reshape(lc * nh, p2), ri.reshape(lc * nh, p2)

    tcol = lax.broadcasted_iota(I32, (lc, 1), 0).astype(F32)
    nr, ni = power(-tcol)
    pr, pi = power(tcol)
    bm_re, bm_im = cmul3(bb_re, bb_im, nr, ni)
    cm_re, cm_im = cmul3(c_re, c_im, pr, pi)
    x = jnp.where(lo, bm_re, -bm_im)
    yc = jnp.where(lo, cm_re, cm_im)
    x_hi, x_lo = _split_bf16(x)
    y_hi, y_lo = _split_bf16(yc)
    t = _dot_nt(x_hi, y_hi) + _dot_nt(x_hi, y_lo) + _dot_nt(x_lo, y_hi)
    m = lc * nh
    ri = lax.broadcasted_iota(I32, (m, m), 0)
    ci = lax.broadcasted_iota(I32, (m, m), 1)
    t = jnp.where(ci // nh >= ri // nh, t, 0.0)
    t = jnp.where(ri == ci, t + d_ref[0], t)
    t_ref[0] = t.astype(t_ref.dtype)

    er, ei = power(jnp.full((1, 1), lc - 1.0, F32))
    bmat_ref[0] = jnp.where(lo, bm_re * er - bm_im * ei, bm_re * ei + bm_im * er).astype(bmat_ref.dtype)
    cmat_ref[0] = jnp.where(lo, cm_re * ab_re - cm_im * ab_im,
                            -(cm_re * ab_im + cm_im * ab_re)).astype(cmat_ref.dtype)
    for i in range(n_levels):
        qr, qi = power(jnp.full((1, 1), float(lc * 2 ** i), F32))
        q_ref[0, i:i + 1, :] = qr
        q_ref[0, n_levels + i:n_levels + i + 1, :] = jnp.where(lo, -qi, qi)


def _s5_conv_body(u_ref, t_ref, bmat_ref, cmat_ref, q_ref, z_ref, *, nc, n_levels):
    u = u_ref[0]
    zloc = jnp.dot(u, bmat_ref[0], preferred_element_type=F32)
    rows, p2 = zloc.shape
    n_in_seq = lax.broadcasted_iota(I32, (rows, p2), 0) % nc
    s = jnp.where(n_in_seq >= 1, pltpu.roll(zloc, 1, 0), 0.0)
    for i in range(n_levels):
        step = 2 ** i
        sh = jnp.where(n_in_seq >= step, pltpu.roll(s, step, 0), 0.0)
        sw = pltpu.roll(sh, p2 // 2, 1)
        s = s + q_ref[0, i:i + 1, :] * sh + q_ref[0, n_levels + i:n_levels + i + 1, :] * sw
    y = jnp.dot(u, t_ref[0], preferred_element_type=F32) + _dot_nt(s.astype(BF16), cmat_ref[0])
    z_ref[0] = jax.nn.gelu(y).astype(z_ref.dtype)


def _s5_branch(p_u, a_re, a_im, log_dt, b_re, b_im, c_re, c_im, d):
    bsz, seq, ds = p_u.shape
    g, p = a_re.shape
    nh = ds // g
    lc = min(S5_CHUNK, seq)
    nc = seq // lc
    n_levels = max(1, (nc - 1).bit_length())
    rows = bsz * nc
    m = lc * nh
    tile2 = lambda a: jnp.concatenate([a, a], axis=-1).astype(F32)
    args = (tile2(a_re)[:, None, :], tile2(a_im)[:, None, :], log_dt.reshape(g, 1, 1).astype(F32),
            tile2(jnp.swapaxes(b_re, 1, 2)), tile2(jnp.swapaxes(b_im, 1, 2)),
            tile2(c_re), tile2(c_im), jnp.tile(d.astype(F32), (1, lc))[:, None, :])
    gspec = lambda a: pl.BlockSpec((1,) + a.shape[1:], lambda i: (i, 0, 0))
    qrows = 2 * n_levels
    t_mat, bmat, cmat, q = pl.pallas_call(
        functools.partial(_s5_param_body, lc=lc, n_levels=n_levels),
        grid=(g,),
        in_specs=[gspec(a) for a in args],
        out_specs=[pl.BlockSpec((1, m, m), lambda i: (i, 0, 0)),
                   pl.BlockSpec((1, m, 2 * p), lambda i: (i, 0, 0)),
                   pl.BlockSpec((1, m, 2 * p), lambda i: (i, 0, 0)),
                   pl.BlockSpec((1, qrows, 2 * p), lambda i: (i, 0, 0))],
        out_shape=[jax.ShapeDtypeStruct((g, m, m), BF16),
                   jax.ShapeDtypeStruct((g, m, 2 * p), BF16),
                   jax.ShapeDtypeStruct((g, m, 2 * p), BF16),
                   jax.ShapeDtypeStruct((g, qrows, 2 * p), F32)],
        compiler_params=_cp(("parallel",)),
        name="s5_params",
    )(*args)

    u = p_u.reshape(bsz, nc, lc, g, nh).transpose(3, 0, 1, 2, 4).reshape(g, rows, m)
    z = pl.pallas_call(
        functools.partial(_s5_conv_body, nc=nc, n_levels=n_levels),
        grid=(g,),
        in_specs=[gspec(u), gspec(t_mat), gspec(bmat), gspec(cmat), gspec(q)],
        out_specs=pl.BlockSpec((1, rows, m), lambda i: (i, 0, 0)),
        out_shape=jax.ShapeDtypeStruct((g, rows, m), BF16),
        compiler_params=_cp(("parallel",)),
        name="s5_conv",
    )(u, t_mat, bmat, cmat, q)
    return z.reshape(g, bsz, nc, lc, nh).transpose(1, 2, 3, 0, 4).reshape(bsz, seq, ds)


def _merge_body(yr_ref, z_ref, wp_ref, w1_ref, w2_ref, gr_ref, gs_ref, o_ref):
    y_rwkv = jnp.dot(yr_ref[...], wp_ref[...], preferred_element_type=F32)
    z = z_ref[...]
    y_s5 = (jnp.dot(z, w1_ref[...], preferred_element_type=F32)
            * _sigmoid(jnp.dot(z, w2_ref[...], preferred_element_type=F32)))
    o_ref[...] = (_sigmoid(gr_ref[...].astype(F32)) * y_rwkv
                  + _sigmoid(gs_ref[...].astype(F32)) * y_s5).astype(o_ref.dtype)


def _merge(yr, z, wp, w1, w2, p_gates, *, tm=512, tn=512):
    t, kr = yr.shape
    ks = z.shape[1]
    d = wp.shape[1]
    tm, tn = min(tm, t), min(tn, d)
    nj = d // tn
    return pl.pallas_call(
        _merge_body,
        grid=(t // tm, nj),
        in_specs=[pl.BlockSpec((tm, kr), lambda i, j: (i, 0)),
                  pl.BlockSpec((tm, ks), lambda i, j: (i, 0)),
                  pl.BlockSpec((kr, tn), lambda i, j: (0, j)),
                  pl.BlockSpec((ks, tn), lambda i, j: (0, j)),
                  pl.BlockSpec((ks, tn), lambda i, j: (0, j)),
                  pl.BlockSpec((tm, tn), lambda i, j: (i, j)),
                  pl.BlockSpec((tm, tn), lambda i, j: (i, j + nj))],
        out_specs=pl.BlockSpec((tm, tn), lambda i, j: (i, j)),
        out_shape=jax.ShapeDtypeStruct((t, d), BF16),
        compiler_params=_cp(("parallel", "parallel")),
        name="merge_branches",
    )(yr, z, wp, w1, w2, p_gates, p_gates)


def _router_body(x_ref, mix_ref, g1_ref, lng_ref, lnb_ref, sc_ref, sh_ref, rw_ref, rb_ref,
                 x1_ref, h2_ref, idx_ref, w_ref, *, alpha):
    x1 = _ln(alpha * x_ref[0] + g1_ref[0] * mix_ref[0].astype(F32)) * lng_ref[...] + lnb_ref[...]
    x1_ref[0] = x1
    h2 = _ln(x1) * (1.0 + sc_ref[0]) + sh_ref[0]
    _pack_rows(h2, h2_ref.at[0])
    logits = jnp.dot(h2, rw_ref[...], preferred_element_type=F32, precision=HI) + rb_ref[...]
    lane = lax.broadcasted_iota(I32, logits.shape, 1)
    lane_f = lane.astype(F32)
    idx_out = jnp.zeros(logits.shape, I32)
    val_out = jnp.zeros(logits.shape, F32)
    vals = []
    for kth in range(TOP_K):
        mx = jnp.max(logits, axis=-1, keepdims=True)
        sel = jnp.min(jnp.where(logits == mx, lane_f, float(LANES)), axis=-1, keepdims=True).astype(I32)
        vals.append(mx)
        idx_out = jnp.where(lane == kth, sel, idx_out)
        logits = jnp.where(lane == sel, -jnp.inf, logits)
    exps = [jnp.exp(vv - vals[0]) for vv in vals]
    tot = exps[0]
    for e in exps[1:]:
        tot = tot + e
    for kth in range(TOP_K):
        val_out = jnp.where(lane == kth, exps[kth] / tot, val_out)
    idx_ref[0] = idx_out
    w_ref[0] = val_out


def _resid_ln_router(x, mix, mod3, ln_g, ln_b, router_w, router_b, *, alpha, tl=256):
    bsz, seq, d = x.shape
    e = router_w.shape[1]
    tl = min(tl, seq)
    ns, lw = _pack_geometry(d)
    rw = jnp.zeros((d, LANES), F32).at[:, :e].set(router_w.astype(F32))
    rb = jnp.full((1, LANES), -jnp.inf, F32).at[0, :e].set(router_b.astype(F32))
    tok = lambda w: pl.BlockSpec((1, tl, w), lambda b, l: (b, l, 0))
    modspec = lambda j: pl.BlockSpec((1, 1, d), lambda b, l: (b * 6 + j, 0, 0))
    full = lambda a: pl.BlockSpec(a.shape, lambda b, l: (0,) * a.ndim)
    args = (x, mix, mod3, ln_g[None].astype(F32), ln_b[None].astype(F32), mod3, mod3, rw, rb)
    return pl.pallas_call(
        functools.partial(_router_body, alpha=alpha),
        grid=(bsz, seq // tl),
        in_specs=[tok(d), tok(d), modspec(2), full(args[3]), full(args[4]), modspec(4), modspec(3),
                  full(rw), full(rb)],
        out_specs=[tok(d), pl.BlockSpec((1, tl * ns, lw), lambda b, l: (b, l, 0)),
                   tok(LANES), tok(LANES)],
        out_shape=[jax.ShapeDtypeStruct((bsz, seq, d), F32),
                   jax.ShapeDtypeStruct((bsz, seq * ns, lw), U32),
                   jax.ShapeDtypeStruct((bsz, seq, LANES), I32),
                   jax.ShapeDtypeStruct((bsz, seq, LANES), F32)],
        compiler_params=_cp(("parallel", "parallel")),
        name="resid_ln_router",
    )(*args)


def _gather_body(idx_ref, src_ref, o_ref, buf, gather_sem, write_sem, *, rows_per_step, n_steps):
    def row_copy(step, slot, r):
        return pltpu.make_async_copy(src_ref.at[idx_ref[step * rows_per_step + r]],
                                     buf.at[slot, r], gather_sem.at[slot])

    def start_gather(step, slot):
        def body(r, carry):
            row_copy(step, slot, r).start()
            return carry
        lax.fori_loop(0, rows_per_step, body, 0, unroll=8)

    def wait_gather(step, slot):
        def body(r, carry):
            row_copy(step, slot, r).wait()
            return carry
        lax.fori_loop(0, rows_per_step, body, 0, unroll=8)

    def write_back(step, slot):
        return pltpu.make_async_copy(buf.at[slot], o_ref.at[pl.ds(step * rows_per_step, rows_per_step)],
                                     write_sem.at[slot])

    i = pl.program_id(0)
    slot = i % 2
    other = 1 - slot

    @pl.when(i == 0)
    def _():
        start_gather(0, 0)

    @pl.when(i >= 1)
    def _():
        write_back(i - 1, other).wait()

    @pl.when(i + 1 < n_steps)
    def _():
        start_gather(i + 1, other)

    wait_gather(i, slot)
    write_back(i, slot).start()

    @pl.when(i == n_steps - 1)
    def _():
        write_back(i, slot).wait()


def _gather_rows(src, idx, *, rows_per_step=256):
    n = idx.shape[0]
    rows_per_step = min(rows_per_step, n)
    assert n % rows_per_step == 0 and src.ndim == 3 and src.shape[2] % LANES == 0
    return pl.pallas_call(
        functools.partial(_gather_body, rows_per_step=rows_per_step, n_steps=n // rows_per_step),
        grid_spec=pltpu.PrefetchScalarGridSpec(
            num_scalar_prefetch=1,
            grid=(n // rows_per_step,),
            in_specs=[pl.BlockSpec(memory_space=pl.ANY)],
            out_specs=pl.BlockSpec(memory_space=pl.ANY),
            scratch_shapes=[pltpu.VMEM((2, rows_per_step) + src.shape[1:], src.dtype),
                            pltpu.SemaphoreType.DMA((2,)), pltpu.SemaphoreType.DMA((2,))],
        ),
        out_shape=jax.ShapeDtypeStruct((n,) + src.shape[1:], src.dtype),
        compiler_params=_cp(("arbitrary",)),
        name="gather_rows",
    )(idx, src)


def _gather_unpack_body(idx_ref, src_ref, o_ref, buf0, buf1, stage, gather_sem, write_sem, *,
                        rows_per_step, n_steps, ns):
    bufs = (buf0, buf1)

    def row_copy(step, slot, r):
        dst = bufs[slot].at[pl.ds(pl.multiple_of(r * ns, ns), ns)]
        return pltpu.make_async_copy(src_ref.at[idx_ref[step * rows_per_step + r]], dst,
                                     gather_sem.at[slot])

    def start_gather(step, slot):
        def body(r, carry):
            row_copy(step, slot, r).start()
            return carry
        lax.fori_loop(0, rows_per_step, body, 0, unroll=8)

    def wait_gather(step, slot):
        def body(r, carry):
            row_copy(step, slot, r).wait()
            return carry
        lax.fori_loop(0, rows_per_step, body, 0, unroll=8)

    def write_back(step, slot):
        return pltpu.make_async_copy(stage.at[slot], o_ref.at[pl.ds(step * rows_per_step, rows_per_step)],
                                     write_sem.at[slot])

    i = pl.program_id(0)

    @pl.when(i == 0)
    def _():
        start_gather(0, 0)

    for slot in (0, 1):
        other = 1 - slot

        @pl.when(i % 2 == slot)
        def _():
            @pl.when(i >= 1)
            def _():
                write_back(i - 1, other).wait()

            @pl.when(i + 1 < n_steps)
            def _():
                start_gather(i + 1, other)

            wait_gather(i, slot)
            stage[slot] = _unpack_rows(bufs[slot], rows_per_step).astype(BF16)
            write_back(i, slot).start()

            @pl.when(i == n_steps - 1)
            def _():
                write_back(i, slot).wait()


def _gather_rows_unpacked(src, idx, d, *, rows_per_step=256):
    n = idx.shape[0]
    ns = src.shape[1]
    rows_per_step = min(rows_per_step, n)
    assert n % rows_per_step == 0 and src.shape[2] == LANES and d == 2 * ns * LANES
    return pl.pallas_call(
        functools.partial(_gather_unpack_body, rows_per_step=rows_per_step,
                          n_steps=n // rows_per_step, ns=ns),
        grid_spec=pltpu.PrefetchScalarGridSpec(
            num_scalar_prefetch=1,
            grid=(n // rows_per_step,),
            in_specs=[pl.BlockSpec(memory_space=pl.ANY)],
            out_specs=pl.BlockSpec(memory_space=pl.ANY),
            scratch_shapes=[pltpu.VMEM((rows_per_step * ns, LANES), src.dtype),
                            pltpu.VMEM((rows_per_step * ns, LANES), src.dtype),
                            pltpu.VMEM((2, rows_per_step, d), BF16),
                            pltpu.SemaphoreType.DMA((2,)), pltpu.SemaphoreType.DMA((2,))],
        ),
        out_shape=jax.ShapeDtypeStruct((n, d), BF16),
        compiler_params=_cp(("arbitrary",)),
        name="gather_rows_unpack",
    )(idx, src)


def _expert_changed(be_ref, i):
    return jnp.logical_or(i == 0, be_ref[i] != be_ref[jnp.maximum(i - 1, 0)])


def _expert_up_body(be_ref, x_ref, wg_ref, bg_ref, wu_ref, bu_ref, o_ref, wg_s, wu_s):
    @pl.when(_expert_changed(be_ref, pl.program_id(1)))
    def _():
        wg_s[...] = wg_ref[0].astype(BF16)
        wu_s[...] = wu_ref[0].astype(BF16)

    x = x_ref[...]
    gate = jnp.dot(x, wg_s[...], preferred_element_type=F32) + bg_ref[0]
    up = jnp.dot(x, wu_s[...], preferred_element_type=F32) + bu_ref[0]
    gate = jnp.minimum(gate, SWIGLU_LIMIT)
    up = jnp.clip(up, -SWIGLU_LIMIT, SWIGLU_LIMIT)
    o_ref[...] = ((up + 1.0) * gate * _sigmoid(SWIGLU_ALPHA * gate)).astype(o_ref.dtype)


def _expert_down_body(be_ref, a_ref, wd_ref, bd_ref, o_ref, wd_s):
    @pl.when(_expert_changed(be_ref, pl.program_id(0)))
    def _():
        wd_s[...] = wd_ref[0].astype(BF16)

    _pack_rows(jnp.dot(a_ref[...], wd_s[...], preferred_element_type=F32) + bd_ref[0], o_ref)


def _experts(xs, block_e, w_gate, b_gate, w_up, b_up, w_down, b_down, *, tf=512):
    e, d, f = w_gate.shape
    ns, lw = _pack_geometry(d)
    n_rows = xs.shape[0]
    nb = n_rows // EXPERT_BLOCK
    tf = _tile(f, tf, LANES)
    packed_rows = lambda imap: pl.BlockSpec((EXPERT_BLOCK * ns, lw), imap)
    act = pl.pallas_call(
        _expert_up_body,
        grid_spec=pltpu.PrefetchScalarGridSpec(
            num_scalar_prefetch=1, grid=(f // tf, nb),
            in_specs=[pl.BlockSpec((EXPERT_BLOCK, d), lambda j, i, be: (i, 0)),
                      pl.BlockSpec((1, d, tf), lambda j, i, be: (be[i], 0, j)),
                      pl.BlockSpec((1, 1, tf), lambda j, i, be: (be[i], 0, j)),
                      pl.BlockSpec((1, d, tf), lambda j, i, be: (be[i], 0, j)),
                      pl.BlockSpec((1, 1, tf), lambda j, i, be: (be[i], 0, j))],
            out_specs=pl.BlockSpec((EXPERT_BLOCK, tf), lambda j, i, be: (i, j)),
            scratch_shapes=[pltpu.VMEM((d, tf), BF16), pltpu.VMEM((d, tf), BF16)]),
        out_shape=jax.ShapeDtypeStruct((n_rows, f), BF16),
        compiler_params=_cp(("arbitrary", "arbitrary")),
        name="expert_up",
    )(block_e, xs, w_gate, b_gate[:, None, :].astype(F32), w_up, b_up[:, None, :].astype(F32))
    return pl.pallas_call(
        _expert_down_body,
        grid_spec=pltpu.PrefetchScalarGridSpec(
            num_scalar_prefetch=1, grid=(nb,),
            in_specs=[pl.BlockSpec((EXPERT_BLOCK, f), lambda i, be: (i, 0)),
                      pl.BlockSpec((1, f, d), lambda i, be: (be[i], 0, 0)),
                      pl.BlockSpec((1, 1, d), lambda i, be: (be[i], 0, 0))],
            out_specs=packed_rows(lambda i, be: (i, 0)),
            scratch_shapes=[pltpu.VMEM((f, d), BF16)]),
        out_shape=jax.ShapeDtypeStruct((n_rows * ns, lw), U32),
        compiler_params=_cp(("arbitrary",)),
        name="expert_down",
    )(block_e, act, w_down, b_down[:, None, :].astype(F32))


def _final_body(x1_ref, y_ref, w_ref, g2_ref, lng_ref, lnb_ref, o_ref, *, alpha):
    wts = w_ref[0]
    ff = None
    for kth in range(TOP_K):
        term = wts[:, kth:kth + 1] * _unpack_rows(y_ref.at[kth], wts.shape[0])
        ff = term if ff is None else ff + term
    o_ref[0] = (_ln(alpha * x1_ref[0] + g2_ref[0] * ff) * lng_ref[...] + lnb_ref[...]).astype(o_ref.dtype)


def _final(x1, ysel, top_w, mod3, ln_g, ln_b, out_dtype, *, alpha, tl=256):
    bsz, seq, d = x1.shape
    ns, lw = _pack_geometry(d)
    tl = min(tl, seq)
    nl = seq // tl
    full = lambda a: pl.BlockSpec(a.shape, lambda b, l: (0,) * a.ndim)
    lg, lb = ln_g[None].astype(F32), ln_b[None].astype(F32)
    return pl.pallas_call(
        functools.partial(_final_body, alpha=alpha),
        grid=(bsz, nl),
        in_specs=[pl.BlockSpec((1, tl, d), lambda b, l: (b, l, 0)),
                  pl.BlockSpec((TOP_K, tl * ns, lw), lambda b, l: (0, b * nl + l, 0)),
                  pl.BlockSpec((1, tl, LANES), lambda b, l: (b, l, 0)),
                  pl.BlockSpec((1, 1, d), lambda b, l: (b * 6 + 5, 0, 0)),
                  full(lg), full(lb)],
        out_specs=pl.BlockSpec((1, tl, d), lambda b, l: (b, l, 0)),
        out_shape=jax.ShapeDtypeStruct((bsz, seq, d), out_dtype),
        compiler_params=_cp(("parallel", "parallel")),
        name="combine_ln",
    )(x1, ysel, top_w, mod3, lg, lb)


def _route(top_idx, n_experts):
    n_tok = top_idx.shape[0]
    n_assign = n_tok * TOP_K
    flat_e = top_idx.reshape(n_assign)
    iota = jnp.arange(n_assign, dtype=I32)
    _, order = lax.sort_key_val(flat_e, iota)
    _, inv_order = lax.sort_key_val(order, iota)
    experts = jnp.arange(n_experts, dtype=I32)
    onehot = flat_e[:, None] == experts[None, :]
    counts = jnp.sum(onehot, axis=0, dtype=I32)
    padded = (counts + EXPERT_BLOCK - 1) // EXPERT_BLOCK * EXPERT_BLOCK
    start = jnp.cumsum(counts) - counts
    pad_end = jnp.cumsum(padded)
    pad_start = pad_end - padded
    shift = jnp.sum(jnp.where(onehot, (pad_start - start)[None, :], 0), axis=1, dtype=I32)
    pos = inv_order + shift
    pos_kmajor = pos.reshape(n_tok, TOP_K).T.reshape(n_assign)
    n_blocks = -(-(n_assign + n_experts * (EXPERT_BLOCK - 1)) // EXPERT_BLOCK)
    block_start = jnp.arange(n_blocks, dtype=I32) * EXPERT_BLOCK
    block_e = jnp.minimum(jnp.sum(pad_end[None, :] <= block_start[:, None], axis=1, dtype=I32),
                          n_experts - 1)
    src_start = jnp.clip(block_start - (pad_start - start)[block_e], 0, n_assign)
    tok_sorted = jnp.concatenate([order // TOP_K, jnp.zeros((EXPERT_BLOCK,), I32)])
    row_src = src_start[:, None] + jnp.arange(EXPERT_BLOCK, dtype=I32)[None, :]
    row_tok = jnp.take(tok_sorted, row_src.reshape(n_blocks * EXPERT_BLOCK))
    return row_tok, block_e, pos_kmajor


def kernel(x, c, ada_w, ada_b, w_in, shift_mu, rwkv_w0, rwkv_w2, rwkv_a0, rwkv_a2, rwkv_g2,
           rwkv_k_k, rwkv_k_a, rwkv_r_k, rwkv_ln_g, rwkv_ln_b, rwkv_proj,
           s5_a_re, s5_a_im, s5_log_dt, s5_b_re, s5_b_im, s5_c_re, s5_c_im, s5_d,
           s5_glu_w1, s5_glu_w2, w_out, ln1_g, ln1_b, ln2_g, ln2_b, router_w, router_b,
           exp_w_gate, exp_b_gate, exp_w_up, exp_b_up, exp_w_down, exp_b_down):
    out_dtype = x.dtype
    bsz, seq, d = x.shape
    depth = ada_w.shape[0]
    alpha = (2 * depth) ** 0.25
    dr = rwkv_w0.shape[1]
    ds = s5_glu_w1.shape[1]
    shift_w = shift_mu.shape[1]
    n_tok = bsz * seq
    n_experts = router_w.shape[2]
    x = x.astype(F32)
    c_pad = jnp.zeros((8, d), F32).at[:bsz].set(c.astype(F32))

    for l in range(depth):
        mod = _matmul(c_pad, ada_w[l], ada_b[l][None].astype(F32), out_dtype=F32, tm=8, tn=512,
                      name="adaln", silu_a=True)[:bsz]
        mod3 = mod.reshape(bsz * 6, 1, d)

        h = _ln_mod(x, mod3, 1, 0).reshape(n_tok, d)
        wi = w_in[l].astype(BF16)
        proj = lambda lo, hi, tn, name: _matmul(h, wi[:, lo:hi], out_dtype=BF16, tm=1024, tn=tn,
                                                name=name)
        p_rkv = proj(0, 3 * dr, 512, "proj_rkv").reshape(bsz, seq, 3 * dr)
        p_lora = proj(3 * dr, shift_w, 256, "proj_lora").reshape(bsz, seq, shift_w - 3 * dr)
        p_u = proj(shift_w, shift_w + ds, 512, "proj_s5").reshape(bsz, seq, ds)
        p_gates = proj(shift_w + ds, shift_w + ds + 2 * d, 512, "proj_gates")

        r, k, v, a, lw, g = _rwkv_prep(p_rkv, p_lora, shift_mu[l].astype(F32), rwkv_w0[l].astype(F32),
                                       rwkv_w2[l], rwkv_a0[l].astype(F32), rwkv_a2[l], rwkv_g2[l])
        yr = _rwkv_recurrence(r, k, v, a, lw, g, rwkv_k_k[l], rwkv_k_a[l], rwkv_r_k[l],
                              rwkv_ln_g[l], rwkv_ln_b[l])
        z = _s5_branch(p_u, s5_a_re[l], s5_a_im[l], s5_log_dt[l], s5_b_re[l], s5_b_im[l],
                       s5_c_re[l], s5_c_im[l], s5_d[l])
        merged = _merge(yr.reshape(n_tok, dr), z.reshape(n_tok, ds), rwkv_proj[l].astype(BF16),
                        s5_glu_w1[l].astype(BF16), s5_glu_w2[l].astype(BF16), p_gates)
        mix = _matmul(merged, w_out[l].astype(BF16), out_dtype=BF16, tm=1024, tn=512, name="w_out")

        x1, h2, top_idx, top_w = _resid_ln_router(x, mix.reshape(bsz, seq, d), mod3, ln1_g[l], ln1_b[l],
                                                  router_w[l], router_b[l], alpha=alpha)
        row_tok, block_e, pos_kmajor = _route(top_idx.reshape(n_tok, LANES)[:, :TOP_K], n_experts)
        ns, lw = _pack_geometry(d)
        xs = _gather_rows_unpacked(h2.reshape(n_tok, ns, lw), row_tok, d)
        ys = _experts(xs, block_e, exp_w_gate[l], exp_b_gate[l], exp_w_up[l],
                      exp_b_up[l], exp_w_down[l], exp_b_down[l])
        ysel = _gather_rows(ys.reshape(-1, ns, lw), pos_kmajor).reshape(TOP_K, n_tok * ns, lw)
        x = _final(x1, ysel, top_w, mod3, ln2_g[l], ln2_b[l], F32, alpha=alpha)
    return x.astype(out_dtype)
```

```python
import functools

import jax
import jax.numpy as jnp
from jax import lax
from jax.experimental import pallas as pl
from jax.experimental.pallas import tpu as pltpu

F32 = jnp.float32
BF16 = jnp.bfloat16
I32 = jnp.int32
U32 = jnp.uint32

LN_EPS = 1e-5
RWKV_GN_EPS = 64e-5
TOP_K = 4
EXPERT_BLOCK = 256
SWIGLU_LIMIT = 7.0
SWIGLU_ALPHA = 1.702
RWKV_CHUNK = 64
S5_CHUNK = 64
LANES = 128
VMEM_LIMIT = 56 * 1024 * 1024
HI = lax.Precision.HIGHEST


def _cp(sem, vmem=VMEM_LIMIT):
    return pltpu.CompilerParams(dimension_semantics=sem, vmem_limit_bytes=vmem)


def _dot(a, b):
    return jnp.dot(a.astype(BF16), b.astype(BF16), preferred_element_type=F32)


def _dot_nt(a, b, precision=None):
    return lax.dot_general(a, b, (((1,), (1,)), ((), ())), preferred_element_type=F32,
                           precision=precision)


def _dot_tn(a, b):
    return lax.dot_general(a.astype(BF16), b.astype(BF16), (((0,), (0,)), ((), ())),
                           preferred_element_type=F32)


def _ln(x):
    mean = jnp.mean(x, axis=-1, keepdims=True)
    xc = x - mean
    var = jnp.mean(xc * xc, axis=-1, keepdims=True)
    return xc * lax.rsqrt(var + LN_EPS)


def _sigmoid(x):
    return 1.0 / (1.0 + jnp.exp(-x))


def _tile(n, pref, align):
    if n <= pref:
        return n
    t = pref - pref % align
    while n % t:
        t -= align
    return t


def _pack_geometry(d):
    return d // (2 * LANES), LANES


def _pack_rows(x, o_ref):
    rows, d = x.shape
    half = d // 2
    ns, lw = o_ref.shape[0] // rows, o_ref.shape[1]
    lo = lax.bitcast_convert_type(x[:, :half].astype(BF16).astype(F32), U32) >> 16
    hi = lax.bitcast_convert_type(x[:, half:].astype(BF16).astype(F32), U32) & jnp.uint32(0xFFFF0000)
    words = hi | lo
    for s in range(ns):
        o_ref[pl.ds(s, rows, stride=ns), :] = words[:, s * lw:(s + 1) * lw]


def _unpack_rows(x_ref, rows):
    ns = x_ref.shape[0] // rows
    los, his = [], []
    for s in range(ns):
        words = x_ref[pl.ds(s, rows, stride=ns), :]
        los.append(lax.bitcast_convert_type(words << 16, F32))
        his.append(lax.bitcast_convert_type(words & jnp.uint32(0xFFFF0000), F32))
    return jnp.concatenate(los + his, axis=-1)


def _mm_body(a_ref, b_ref, bias_ref, o_ref, *, silu_a):
    a = a_ref[...]
    if silu_a:
        a = a * _sigmoid(a)
    o_ref[...] = (_dot(a, b_ref[...]) + bias_ref[...]).astype(o_ref.dtype)


def _matmul(a, b, bias=None, *, out_dtype, tm, tn, name, silu_a=False):
    m, k = a.shape
    n = b.shape[1]
    tm, tn = _tile(m, tm, 8), _tile(n, tn, LANES)
    if bias is None:
        bias = jnp.zeros((1, n), F32)
    return pl.pallas_call(
        functools.partial(_mm_body, silu_a=silu_a),
        grid=(m // tm, n // tn),
        in_specs=[pl.BlockSpec((tm, k), lambda i, j: (i, 0)),
                  pl.BlockSpec((k, tn), lambda i, j: (0, j)),
                  pl.BlockSpec((1, tn), lambda i, j: (0, j))],
        out_specs=pl.BlockSpec((tm, tn), lambda i, j: (i, j)),
        out_shape=jax.ShapeDtypeStruct((m, n), out_dtype),
        compiler_params=_cp(("parallel", "parallel")),
        name=name,
    )(a, b, bias)


def _ln_mod_body(x_ref, sc_ref, sh_ref, o_ref):
    y = _ln(x_ref[0])
    o_ref[0] = (y * (1.0 + sc_ref[0]) + sh_ref[0]).astype(o_ref.dtype)


def _ln_mod(x, mod3, sc_idx, sh_idx, *, tl=256):
    bsz, seq, d = x.shape
    tl = min(tl, seq)
    return pl.pallas_call(
        _ln_mod_body,
        grid=(bsz, seq // tl),
        in_specs=[pl.BlockSpec((1, tl, d), lambda b, l: (b, l, 0)),
                  pl.BlockSpec((1, 1, d), lambda b, l: (b * 6 + sc_idx, 0, 0)),
                  pl.BlockSpec((1, 1, d), lambda b, l: (b * 6 + sh_idx, 0, 0))],
        out_specs=pl.BlockSpec((1, tl, d), lambda b, l: (b, l, 0)),
        out_shape=jax.ShapeDtypeStruct((bsz, seq, d), BF16),
        compiler_params=_cp(("parallel", "parallel")),
        name="ln_modulate",
    )(x, mod3, mod3)


def _token_shift(x, carry_ref, mu):
    rolled = pltpu.roll(x, 1, 0)
    row = lax.broadcasted_iota(I32, x.shape, 0)
    prev = jnp.where(row == 0, carry_ref[...], rolled)
    carry_ref[...] = x[x.shape[0] - 1:, :]
    return x + (prev - x) * mu


def _rwkv_prep_body(prkv_ref, plora_ref, mu_rkv_ref, mu_lora_ref, w0_ref, w2_ref, a0_ref, a2_ref,
                    g2_ref, r_ref, k_ref, v_ref, a_ref, lw_ref, g_ref, carry_rkv, carry_lora,
                    *, dr, w_lora, a_lora):
    @pl.when(pl.program_id(1) == 0)
    def _():
        carry_rkv[...] = jnp.zeros_like(carry_rkv)
        carry_lora[...] = jnp.zeros_like(carry_lora)

    xs = _token_shift(prkv_ref[0].astype(F32), carry_rkv, mu_rkv_ref[...])
    r_ref[0] = xs[:, :dr].astype(r_ref.dtype)
    k_ref[0] = xs[:, dr:2 * dr].astype(k_ref.dtype)
    v_ref[0] = xs[:, 2 * dr:].astype(v_ref.dtype)

    ls = _token_shift(plora_ref[0].astype(F32), carry_lora, mu_lora_ref[...])
    xw = ls[:, :w_lora]
    xa = ls[:, w_lora:w_lora + a_lora]
    xg = ls[:, w_lora + a_lora:]
    z = -(w0_ref[...] + _dot(jnp.tanh(xw), w2_ref[...]))
    softplus = jnp.maximum(z, 0.0) + jnp.log1p(jnp.exp(-jnp.abs(z)))
    lw_ref[0] = -jnp.exp(-softplus - 0.5)
    a_ref[0] = _sigmoid(a0_ref[...] + _dot(xa, a2_ref[...])).astype(a_ref.dtype)
    g_ref[0] = _dot(_sigmoid(xg), g2_ref[...]).astype(g_ref.dtype)


def _rwkv_prep(p_rkv, p_lora, mu, w0, w2, a0, a2, g2, *, tl=256):
    bsz, seq, dr3 = p_rkv.shape
    dr = dr3 // 3
    lora = p_lora.shape[-1]
    w_lora, a_lora = w2.shape[0], a2.shape[0]
    tl = min(tl, seq)
    tok = lambda w: pl.BlockSpec((1, tl, w), lambda b, l: (b, l, 0))
    full = lambda a: pl.BlockSpec(a.shape, lambda b, l: (0,) * a.ndim)
    mu_rkv, mu_lora = mu[None, :dr3], mu[None, dr3:]
    args = (p_rkv, p_lora, mu_rkv, mu_lora, w0[None], w2.astype(BF16), a0[None], a2.astype(BF16),
            g2.astype(BF16))
    out = jax.ShapeDtypeStruct((bsz, seq, dr), BF16)
    return pl.pallas_call(
        functools.partial(_rwkv_prep_body, dr=dr, w_lora=w_lora, a_lora=a_lora),
        grid=(bsz, seq // tl),
        in_specs=[tok(dr3), tok(lora)] + [full(a) for a in args[2:]],
        out_specs=[tok(dr)] * 6,
        out_shape=[out, out, out, out, jax.ShapeDtypeStruct((bsz, seq, dr), F32), out],
        scratch_shapes=[pltpu.VMEM((1, dr3), F32), pltpu.VMEM((1, lora), F32)],
        compiler_params=_cp(("parallel", "arbitrary")),
        name="rwkv_prep",
    )(*args)


def _split_bf16(x):
    hi = x.astype(BF16)
    return hi, (x - hi.astype(F32)).astype(BF16)


def _head_sum(x, bd_ref):
    c, w = x.shape
    gw = bd_ref.shape[0]
    ng = w // gw
    hi, lo = _split_bf16(x)
    groups = [p[:, j * gw:(j + 1) * gw] for p in (hi, lo) for j in range(ng)]
    res = jnp.dot(jnp.concatenate(groups, axis=0), bd_ref[...], preferred_element_type=F32)
    parts = [res[j * c:(j + 1) * c] + res[(ng + j) * c:(ng + j + 1) * c] for j in range(ng)]
    return jnp.concatenate(parts, axis=-1)


def _rwkv_rec_body(r_ref, k_ref, v_ref, a_ref, lw_ref, g_ref, kk_ref, ka_ref, rk_ref, lng_ref,
                   lnb_ref, bd_ref, y_ref, state, ybuf, *, heads, n):
    c = r_ref.shape[1]

    @pl.when(pl.program_id(2) == 0)
    def _():
        state[...] = jnp.zeros_like(state)

    row = lax.broadcasted_iota(I32, (c, c), 0)
    col = lax.broadcasted_iota(I32, (c, c), 1)
    strict = row > col
    eye = (row == col).astype(F32)
    tril_b = (row >= col).astype(BF16)
    row2 = lax.broadcasted_iota(I32, (c, 2 * c), 0)
    col2 = lax.broadcasted_iota(I32, (c, 2 * c), 1)
    strict_right = (col2 >= c) & (row2 > col2 - c)
    lower_pair = row2 >= jnp.where(col2 >= c, col2 - c, col2)

    r = r_ref[0].astype(F32)
    k = k_ref[0].astype(F32)
    v = v_ref[0].astype(F32)
    a = a_ref[0].astype(F32)
    lw = lw_ref[0]
    lw_hi, lw_lo = _split_bf16(lw)
    cl = (jnp.dot(tril_b, lw_hi, preferred_element_type=F32)
          + jnp.dot(tril_b, lw_lo, preferred_element_type=F32))
    w_inc = jnp.exp(cl)
    w_inv = jnp.exp(-cl)
    w_prev = jnp.exp(cl - lw)
    kk = k * kk_ref[...]
    kk = kk / jnp.maximum(jnp.sqrt(_head_sum(kk * kk, bd_ref)), 1e-12)
    k2 = k * (1.0 + (a - 1.0) * ka_ref[...])
    rt = (r * w_inc).astype(BF16)
    kkt = (kk * w_prev).astype(BF16)
    bt = (kk * a * w_inv).astype(BF16)
    kt = (k2 * w_inv).astype(BF16)
    vb = v.astype(BF16)
    bonus = _head_sum(r * k2 * rk_ref[...], bd_ref) * v

    hs = range(heads)
    sls = [slice(h * n, (h + 1) * n) for h in hs]
    lhs2 = [jnp.concatenate([kkt[:, sl], rt[:, sl]], axis=0) for sl in sls]
    rhs2 = [jnp.concatenate([bt[:, sl], kt[:, sl]], axis=0) for sl in sls]
    vh = [vb[:, sl] for sl in sls]
    s0 = [state[h] for h in hs]
    gram = [_dot_nt(lhs2[h], rhs2[h]) for h in hs]
    m1 = [_dot_nt(lhs2[h], s0[h].astype(BF16)) for h in hs]
    l_b = [jnp.where(strict, gram[h][:c, :c], 0.0) for h in hs]
    lk_only = [jnp.where(strict_right, gram[h][:c], 0.0).astype(BF16) for h in hs]
    p_bk = [jnp.where(lower_pair, gram[h][c:], 0.0).astype(BF16) for h in hs]
    rhs_u = [m1[h][:c] + jnp.dot(lk_only[h], jnp.concatenate([vh[h], vh[h]], axis=0),
                                 preferred_element_type=F32) for h in hs]

    t_inv = [eye - l_b[h] for h in hs]
    pw = [_dot(l_b[h], l_b[h]) for h in hs]
    span = 4
    while span < c:
        both = [_dot(jnp.concatenate([t_inv[h], pw[h]], axis=0), pw[h]) for h in hs]
        t_inv = [t_inv[h] + both[h][:c] for h in hs]
        pw = [both[h][c:] for h in hs]
        span *= 2
    t_inv = [t_inv[h] + _dot(t_inv[h], pw[h]) for h in hs]

    u = [-_dot(t_inv[h], rhs_u[h]) for h in hs]
    uv = [jnp.concatenate([u[h].astype(BF16), vh[h]], axis=0) for h in hs]
    y_h = [m1[h][c:] + jnp.dot(p_bk[h], uv[h], preferred_element_type=F32) for h in hs]
    s_new = [s0[h] + _dot_tn(uv[h], rhs2[h]) for h in hs]
    for h in hs:
        ybuf[:, sls[h]] = y_h[h]
        state[h] = s_new[h] * w_inc[c - 1:, sls[h]]

    y = ybuf[...]
    inv_n = 1.0 / n
    yc = y - _head_sum(y, bd_ref) * inv_n
    yv = _head_sum(yc * yc, bd_ref) * inv_n
    yn = yc * lax.rsqrt(yv + RWKV_GN_EPS) * lng_ref[...] + lnb_ref[...]
    y_ref[0] = ((yn + bonus) * g_ref[0].astype(F32)).astype(y_ref.dtype)


def _rwkv_recurrence(r, k, v, a, lw, g, k_k, k_a, r_k, ln_g, ln_b, *, heads_per_step=32):
    bsz, seq, dr = r.shape
    n = k_k.shape[-1]
    hb = min(heads_per_step, dr // n)
    wblk = hb * n
    gw = min(2 * LANES, wblk)
    assert wblk % LANES == 0 and dr % wblk == 0 and seq % RWKV_CHUNK == 0 and wblk % gw == 0
    c = RWKV_CHUNK
    tok = pl.BlockSpec((1, c, wblk), lambda b, h, t: (b, t, h))
    par = pl.BlockSpec((1, wblk), lambda b, h, t: (0, h))
    flat = lambda p: p.reshape(1, dr).astype(F32)
    gi = lax.broadcasted_iota(I32, (gw, gw), 0) // n
    gj = lax.broadcasted_iota(I32, (gw, gw), 1) // n
    block_diag = (gi == gj).astype(BF16)
    return pl.pallas_call(
        functools.partial(_rwkv_rec_body, heads=hb, n=n),
        grid=(bsz, dr // wblk, seq // c),
        in_specs=[tok] * 6 + [par] * 5 + [pl.BlockSpec((gw, gw), lambda b, h, t: (0, 0))],
        out_specs=tok,
        out_shape=jax.ShapeDtypeStruct((bsz, seq, dr), BF16),
        scratch_shapes=[pltpu.VMEM((hb, n, n), F32), pltpu.VMEM((c, wblk), F32)],
        compiler_params=_cp(("parallel", "parallel", "arbitrary")),
        name="rwkv_recurrence",
    )(r, k, v, a, lw, g, flat(k_k), flat(k_a), flat(r_k), flat(ln_g), flat(ln_b), block_diag)


def _s5_param_body(are_ref, aim_ref, ldt_ref, bre_ref, bim_ref, cre_ref, cim_ref, d_ref,
                   t_ref, bmat_ref, cmat_ref, q_ref, *, lc, n_levels):
    a_re, a_im = are_ref[0], aim_ref[0]
    dt = jnp.exp(ldt_ref[0])
    lam = a_re * dt
    ang = a_im * dt
    mag = jnp.exp(lam)
    ab_re, ab_im = mag * jnp.cos(ang), mag * jnp.sin(ang)
    den = a_re * a_re + a_im * a_im
    f_re = ((ab_re - 1.0) * a_re + ab_im * a_im) / den
    f_im = (ab_im * a_re - (ab_re - 1.0) * a_im) / den
    b_re, b_im = bre_ref[0], bim_ref[0]
    bb_re = f_re * b_re - f_im * b_im
    bb_im = f_re * b_im + f_im * b_re
    c_re, c_im = cre_ref[0], cim_ref[0]
    nh, p2 = b_re.shape
    lo = lax.broadcasted_iota(I32, (1, p2), 1) < (p2 // 2)

    def power(tf):
        m = jnp.exp(tf * lam)
        return m * jnp.cos(tf * ang), m * jnp.sin(tf * ang)

    def cmul3(xr, xi, pr, pi):
        rr = xr[None] * pr[:, None, :] - xi[None] * pi[:, None, :]
        ri = xr[None] * pi[:, None, :] + xi[None] * pr[:, None, :]
        return rr.reshape(lc * nh, p2), ri.reshape(lc * nh, p2)

    tcol = lax.broadcasted_iota(I32, (lc, 1), 0).astype(F32)
    nr, ni = power(-tcol)
    pr, pi = power(tcol)
    bm_re, bm_im = cmul3(bb_re, bb_im, nr, ni)
    cm_re, cm_im = cmul3(c_re, c_im, pr, pi)
    x = jnp.where(lo, bm_re, -bm_im)
    yc = jnp.where(lo, cm_re, cm_im)
    x_hi, x_lo = _split_bf16(x)
    y_hi, y_lo = _split_bf16(yc)
    t = _dot_nt(x_hi, y_hi) + _dot_nt(x_hi, y_lo) + _dot_nt(x_lo, y_hi)
    m = lc * nh
    ri = lax.broadcasted_iota(I32, (m, m), 0)
    ci = lax.broadcasted_iota(I32, (m, m), 1)
    t = jnp.where(ci // nh >= ri // nh, t, 0.0)
    t = jnp.where(ri == ci, t + d_ref[0], t)
    t_ref[0] = t.astype(t_ref.dtype)

    er, ei = power(jnp.full((1, 1), lc - 1.0, F32))
    bmat_ref[0] = jnp.where(lo, bm_re * er - bm_im * ei, bm_re * ei + bm_im * er).astype(bmat_ref.dtype)
    cmat_ref[0] = jnp.where(lo, cm_re * ab_re - cm_im * ab_im,
                            -(cm_re * ab_im + cm_im * ab_re)).astype(cmat_ref.dtype)
    for i in range(n_levels):
        qr, qi = power(jnp.full((1, 1), float(lc * 2 ** i), F32))
        q_ref[0, i:i + 1, :] = qr
        q_ref[0, n_levels + i:n_levels + i + 1, :] = jnp.where(lo, -qi, qi)


def _s5_conv_body(u_ref, t_ref, bmat_ref, cmat_ref, q_ref, z_ref, *, nc, n_levels):
    u = u_ref[0]
    zloc = jnp.dot(u, bmat_ref[0], preferred_element_type=F32)
    rows, p2 = zloc.shape
    n_in_seq = lax.broadcasted_iota(I32, (rows, p2), 0) % nc
    s = jnp.where(n_in_seq >= 1, pltpu.roll(zloc, 1, 0), 0.0)
    for i in range(n_levels):
        step = 2 ** i
        sh = jnp.where(n_in_seq >= step, pltpu.roll(s, step, 0), 0.0)
        sw = pltpu.roll(sh, p2 // 2, 1)
        s = s + q_ref[0, i:i + 1, :] * sh + q_ref[0, n_levels + i:n_levels + i + 1, :] * sw
    y = jnp.dot(u, t_ref[0], preferred_element_type=F32) + _dot_nt(s.astype(BF16), cmat_ref[0])
    z_ref[0] = jax.nn.gelu(y).astype(z_ref.dtype)


def _s5_branch(p_u, a_re, a_im, log_dt, b_re, b_im, c_re, c_im, d):
    bsz, seq, ds = p_u.shape
    g, p = a_re.shape
    nh = ds // g
    lc = min(S5_CHUNK, seq)
    nc = seq // lc
    n_levels = max(1, (nc - 1).bit_length())
    rows = bsz * nc
    m = lc * nh
    tile2 = lambda a: jnp.concatenate([a, a], axis=-1).astype(F32)
    args = (tile2(a_re)[:, None, :], tile2(a_im)[:, None, :], log_dt.reshape(g, 1, 1).astype(F32),
            tile2(jnp.swapaxes(b_re, 1, 2)), tile2(jnp.swapaxes(b_im, 1, 2)),
            tile2(c_re), tile2(c_im), jnp.tile(d.astype(F32), (1, lc))[:, None, :])
    gspec = lambda a: pl.BlockSpec((1,) + a.shape[1:], lambda i: (i, 0, 0))
    qrows = 2 * n_levels
    t_mat, bmat, cmat, q = pl.pallas_call(
        functools.partial(_s5_param_body, lc=lc, n_levels=n_levels),
        grid=(g,),
        in_specs=[gspec(a) for a in args],
        out_specs=[pl.BlockSpec((1, m, m), lambda i: (i, 0, 0)),
                   pl.BlockSpec((1, m, 2 * p), lambda i: (i, 0, 0)),
                   pl.BlockSpec((1, m, 2 * p), lambda i: (i, 0, 0)),
                   pl.BlockSpec((1, qrows, 2 * p), lambda i: (i, 0, 0))],
        out_shape=[jax.ShapeDtypeStruct((g, m, m), BF16),
                   jax.ShapeDtypeStruct((g, m, 2 * p), BF16),
                   jax.ShapeDtypeStruct((g, m, 2 * p), BF16),
                   jax.ShapeDtypeStruct((g, qrows, 2 * p), F32)],
        compiler_params=_cp(("parallel",)),
        name="s5_params",
    )(*args)

    u = p_u.reshape(bsz, nc, lc, g, nh).transpose(3, 0, 1, 2, 4).reshape(g, rows, m)
    z = pl.pallas_call(
        functools.partial(_s5_conv_body, nc=nc, n_levels=n_levels),
        grid=(g,),
        in_specs=[gspec(u), gspec(t_mat), gspec(bmat), gspec(cmat), gspec(q)],
        out_specs=pl.BlockSpec((1, rows, m), lambda i: (i, 0, 0)),
        out_shape=jax.ShapeDtypeStruct((g, rows, m), BF16),
        compiler_params=_cp(("parallel",)),
        name="s5_conv",
    )(u, t_mat, bmat, cmat, q)
    return z.reshape(g, bsz, nc, lc, nh).transpose(1, 2, 3, 0, 4).reshape(bsz, seq, ds)


def _merge_body(yr_ref, z_ref, wp_ref, w1_ref, w2_ref, gr_ref, gs_ref, o_ref):
    y_rwkv = jnp.dot(yr_ref[...], wp_ref[...], preferred_element_type=F32)
    z = z_ref[...]
    y_s5 = (jnp.dot(z, w1_ref[...], preferred_element_type=F32)
            * _sigmoid(jnp.dot(z, w2_ref[...], preferred_element_type=F32)))
    o_ref[...] = (_sigmoid(gr_ref[...].astype(F32)) * y_rwkv
                  + _sigmoid(gs_ref[...].astype(F32)) * y_s5).astype(o_ref.dtype)


def _merge(yr, z, wp, w1, w2, p_gates, *, tm=512, tn=512):
    t, kr = yr.shape
    ks = z.shape[1]
    d = wp.shape[1]
    tm, tn = min(tm, t), min(tn, d)
    nj = d // tn
    return pl.pallas_call(
        _merge_body,
        grid=(t // tm, nj),
        in_specs=[pl.BlockSpec((tm, kr), lambda i, j: (i, 0)),
                  pl.BlockSpec((tm, ks), lambda i, j: (i, 0)),
                  pl.BlockSpec((kr, tn), lambda i, j: (0, j)),
                  pl.BlockSpec((ks, tn), lambda i, j: (0, j)),
                  pl.BlockSpec((ks, tn), lambda i, j: (0, j)),
                  pl.BlockSpec((tm, tn), lambda i, j: (i, j)),
                  pl.BlockSpec((tm, tn), lambda i, j: (i, j + nj))],
        out_specs=pl.BlockSpec((tm, tn), lambda i, j: (i, j)),
        out_shape=jax.ShapeDtypeStruct((t, d), BF16),
        compiler_params=_cp(("parallel", "parallel")),
        name="merge_branches",
    )(yr, z, wp, w1, w2, p_gates, p_gates)


def _router_body(x_ref, mix_ref, g1_ref, lng_ref, lnb_ref, sc_ref, sh_ref, rw_ref, rb_ref,
                 x1_ref, h2_ref, idx_ref, w_ref, *, alpha):
    x1 = _ln(alpha * x_ref[0] + g1_ref[0] * mix_ref[0].astype(F32)) * lng_ref[...] + lnb_ref[...]
    x1_ref[0] = x1
    h2 = _ln(x1) * (1.0 + sc_ref[0]) + sh_ref[0]
    _pack_rows(h2, h2_ref.at[0])
    logits = jnp.dot(h2, rw_ref[...], preferred_element_type=F32, precision=HI) + rb_ref[...]
    lane = lax.broadcasted_iota(I32, logits.shape, 1)
    lane_f = lane.astype(F32)
    idx_out = jnp.zeros(logits.shape, I32)
    val_out = jnp.zeros(logits.shape, F32)
    vals = []
    for kth in range(TOP_K):
        mx = jnp.max(logits, axis=-1, keepdims=True)
        sel = jnp.min(jnp.where(logits == mx, lane_f, float(LANES)), axis=-1, keepdims=True).astype(I32)
        vals.append(mx)
        idx_out = jnp.where(lane == kth, sel, idx_out)
        logits = jnp.where(lane == sel, -jnp.inf, logits)
    exps = [jnp.exp(vv - vals[0]) for vv in vals]
    tot = exps[0]
    for e in exps[1:]:
        tot = tot + e
    for kth in range(TOP_K):
        val_out = jnp.where(lane == kth, exps[kth] / tot, val_out)
    idx_ref[0] = idx_out
    w_ref[0] = val_out


def _resid_ln_router(x, mix, mod3, ln_g, ln_b, router_w, router_b, *, alpha, tl=256):
    bsz, seq, d = x.shape
    e = router_w.shape[1]
    tl = min(tl, seq)
    ns, lw = _pack_geometry(d)
    rw = jnp.zeros((d, LANES), F32).at[:, :e].set(router_w.astype(F32))
    rb = jnp.full((1, LANES), -jnp.inf, F32).at[0, :e].set(router_b.astype(F32))
    tok = lambda w: pl.BlockSpec((1, tl, w), lambda b, l: (b, l, 0))
    modspec = lambda j: pl.BlockSpec((1, 1, d), lambda b, l: (b * 6 + j, 0, 0))
    full = lambda a: pl.BlockSpec(a.shape, lambda b, l: (0,) * a.ndim)
    args = (x, mix, mod3, ln_g[None].astype(F32), ln_b[None].astype(F32), mod3, mod3, rw, rb)
    return pl.pallas_call(
        functools.partial(_router_body, alpha=alpha),
        grid=(bsz, seq // tl),
        in_specs=[tok(d), tok(d), modspec(2), full(args[3]), full(args[4]), modspec(4), modspec(3),
                  full(rw), full(rb)],
        out_specs=[tok(d), pl.BlockSpec((1, tl * ns, lw), lambda b, l: (b, l, 0)),
                   tok(LANES), tok(LANES)],
        out_shape=[jax.ShapeDtypeStruct((bsz, seq, d), F32),
                   jax.ShapeDtypeStruct((bsz, seq * ns, lw), U32),
                   jax.ShapeDtypeStruct((bsz, seq, LANES), I32),
                   jax.ShapeDtypeStruct((bsz, seq, LANES), F32)],
        compiler_params=_cp(("parallel", "parallel")),
        name="resid_ln_router",
    )(*args)


def _gather_unpack_body(idx_ref, src_ref, o_ref, buf0, buf1, stage, gather_sem, write_sem, *,
                        rows_per_step, n_steps, ns):
    bufs = (buf0, buf1)

    def row_copy(step, slot, r):
        dst = bufs[slot].at[pl.ds(pl.multiple_of(r * ns, ns), ns)]
        return pltpu.make_async_copy(src_ref.at[idx_ref[step * rows_per_step + r]], dst,
                                     gather_sem.at[slot])

    def start_gather(step, slot):
        def body(r, carry):
            row_copy(step, slot, r).start()
            return carry
        lax.fori_loop(0, rows_per_step, body, 0, unroll=8)

    def wait_gather(step, slot):
        def body(r, carry):
            row_copy(step, slot, r).wait()
            return carry
        lax.fori_loop(0, rows_per_step, body, 0, unroll=8)

    def write_back(step, slot):
        return pltpu.make_async_copy(stage.at[slot], o_ref.at[pl.ds(step * rows_per_step, rows_per_step)],
                                     write_sem.at[slot])

    i = pl.program_id(0)

    @pl.when(i == 0)
    def _():
        start_gather(0, 0)

    for slot in (0, 1):
        other = 1 - slot

        @pl.when(i % 2 == slot)
        def _():
            @pl.when(i >= 1)
            def _():
                write_back(i - 1, other).wait()

            @pl.when(i + 1 < n_steps)
            def _():
                start_gather(i + 1, other)

            wait_gather(i, slot)
            stage[slot] = _unpack_rows(bufs[slot], rows_per_step).astype(BF16)
            write_back(i, slot).start()

            @pl.when(i == n_steps - 1)
            def _():
                write_back(i, slot).wait()


def _gather_rows_unpacked(src, idx, d, *, rows_per_step=256):
    n = idx.shape[0]
    ns = src.shape[1]
    rows_per_step = min(rows_per_step, n)
    assert n % rows_per_step == 0 and src.shape[2] == LANES and d == 2 * ns * LANES
    return pl.pallas_call(
        functools.partial(_gather_unpack_body, rows_per_step=rows_per_step,
                          n_steps=n // rows_per_step, ns=ns),
        grid_spec=pltpu.PrefetchScalarGridSpec(
            num_scalar_prefetch=1,
            grid=(n // rows_per_step,),
            in_specs=[pl.BlockSpec(memory_space=pl.ANY)],
            out_specs=pl.BlockSpec(memory_space=pl.ANY),
            scratch_shapes=[pltpu.VMEM((rows_per_step * ns, LANES), src.dtype),
                            pltpu.VMEM((rows_per_step * ns, LANES), src.dtype),
                            pltpu.VMEM((2, rows_per_step, d), BF16),
                            pltpu.SemaphoreType.DMA((2,)), pltpu.SemaphoreType.DMA((2,))],
        ),
        out_shape=jax.ShapeDtypeStruct((n, d), BF16),
        compiler_params=_cp(("arbitrary",)),
        name="gather_rows_unpack",
    )(idx, src)


def _expert_changed(be_ref, i):
    return jnp.logical_or(i == 0, be_ref[i] != be_ref[jnp.maximum(i - 1, 0)])


def _expert_up_body(be_ref, x_ref, wg_ref, bg_ref, wu_ref, bu_ref, o_ref, wg_s, wu_s):
    @pl.when(_expert_changed(be_ref, pl.program_id(1)))
    def _():
        wg_s[...] = wg_ref[0].astype(BF16)
        wu_s[...] = wu_ref[0].astype(BF16)

    x = x_ref[...]
    gate = jnp.dot(x, wg_s[...], preferred_element_type=F32) + bg_ref[0]
    up = jnp.dot(x, wu_s[...], preferred_element_type=F32) + bu_ref[0]
    gate = jnp.minimum(gate, SWIGLU_LIMIT)
    up = jnp.clip(up, -SWIGLU_LIMIT, SWIGLU_LIMIT)
    o_ref[...] = ((up + 1.0) * gate * _sigmoid(SWIGLU_ALPHA * gate)).astype(o_ref.dtype)


def _expert_down_body(be_ref, a_ref, wd_ref, bd_ref, o_ref, wd_s):
    @pl.when(_expert_changed(be_ref, pl.program_id(0)))
    def _():
        wd_s[...] = wd_ref[0].astype(BF16)

    _pack_rows(jnp.dot(a_ref[...], wd_s[...], preferred_element_type=F32) + bd_ref[0], o_ref)


def _experts(xs, block_e, w_gate, b_gate, w_up, b_up, w_down, b_down, *, tf=512):
    e, d, f = w_gate.shape
    ns, lw = _pack_geometry(d)
    n_rows = xs.shape[0]
    nb = n_rows // EXPERT_BLOCK
    tf = _tile(f, tf, LANES)
    packed_rows = lambda imap: pl.BlockSpec((EXPERT_BLOCK * ns, lw), imap)
    act = pl.pallas_call(
        _expert_up_body,
        grid_spec=pltpu.PrefetchScalarGridSpec(
            num_scalar_prefetch=1, grid=(f // tf, nb),
            in_specs=[pl.BlockSpec((EXPERT_BLOCK, d), lambda j, i, be: (i, 0)),
                      pl.BlockSpec((1, d, tf), lambda j, i, be: (be[i], 0, j)),
                      pl.BlockSpec((1, 1, tf), lambda j, i, be: (be[i], 0, j)),
                      pl.BlockSpec((1, d, tf), lambda j, i, be: (be[i], 0, j)),
                      pl.BlockSpec((1, 1, tf), lambda j, i, be: (be[i], 0, j))],
            out_specs=pl.BlockSpec((EXPERT_BLOCK, tf), lambda j, i, be: (i, j)),
            scratch_shapes=[pltpu.VMEM((d, tf), BF16), pltpu.VMEM((d, tf), BF16)]),
        out_shape=jax.ShapeDtypeStruct((n_rows, f), BF16),
        compiler_params=_cp(("arbitrary", "arbitrary")),
        name="expert_up",
    )(block_e, xs, w_gate, b_gate[:, None, :].astype(F32), w_up, b_up[:, None, :].astype(F32))
    return pl.pallas_call(
        _expert_down_body,
        grid_spec=pltpu.PrefetchScalarGridSpec(
            num_scalar_prefetch=1, grid=(nb,),
            in_specs=[pl.BlockSpec((EXPERT_BLOCK, f), lambda i, be: (i, 0)),
                      pl.BlockSpec((1, f, d), lambda i, be: (be[i], 0, 0)),
                      pl.BlockSpec((1, 1, d), lambda i, be: (be[i], 0, 0))],
            out_specs=packed_rows(lambda i, be: (i, 0)),
            scratch_shapes=[pltpu.VMEM((f, d), BF16)]),
        out_shape=jax.ShapeDtypeStruct((n_rows * ns, lw), U32),
        compiler_params=_cp(("arbitrary",)),
        name="expert_down",
    )(block_e, act, w_down, b_down[:, None, :].astype(F32))


def _final_body(idx_ref, ys_ref, x1_ref, w_ref, g2_ref, lng_ref, lnb_ref, o_ref, buf0, buf1,
                gather_sem, *, alpha, tl, ns, n_steps):
    bufs = (buf0, buf1)
    rows = TOP_K * tl

    def row_copy(step, slot, r):
        dst = bufs[slot].at[pl.ds(pl.multiple_of(r * ns, ns), ns)]
        return pltpu.make_async_copy(ys_ref.at[idx_ref[step * rows + r]], dst, gather_sem.at[slot])

    def start_gather(step, slot):
        def body(r, carry):
            row_copy(step, slot, r).start()
            return carry
        lax.fori_loop(0, rows, body, 0, unroll=8)

    def wait_gather(step, slot):
        def body(r, carry):
            row_copy(step, slot, r).wait()
            return carry
        lax.fori_loop(0, rows, body, 0, unroll=8)

    i = pl.program_id(0)

    @pl.when(i == 0)
    def _():
        start_gather(0, 0)

    for slot in (0, 1):
        @pl.when(i % 2 == slot)
        def _():
            @pl.when(i + 1 < n_steps)
            def _():
                start_gather(i + 1, 1 - slot)

            wait_gather(i, slot)
            wts = w_ref[0]
            ff = None
            for kth in range(TOP_K):
                y_k = _unpack_rows(bufs[slot].at[pl.ds(kth * tl * ns, tl * ns)], tl)
                term = wts[:, kth:kth + 1] * y_k
                ff = term if ff is None else ff + term
            o_ref[0] = (_ln(alpha * x1_ref[0] + g2_ref[0] * ff) * lng_ref[...]
                        + lnb_ref[...]).astype(o_ref.dtype)


def _final(x1, ys, pos, top_w, mod3, ln_g, ln_b, out_dtype, *, alpha, tl=256):
    bsz, seq, d = x1.shape
    ns, lw = _pack_geometry(d)
    tl = min(tl, seq)
    nl = seq // tl
    n_steps = bsz * nl
    idx = pos.reshape(n_steps, tl, TOP_K).transpose(0, 2, 1).reshape(-1)
    full = lambda a: pl.BlockSpec(a.shape, lambda i, idx_ref: (0,) * a.ndim)
    lg, lb = ln_g[None].astype(F32), ln_b[None].astype(F32)
    return pl.pallas_call(
        functools.partial(_final_body, alpha=alpha, tl=tl, ns=ns, n_steps=n_steps),
        grid_spec=pltpu.PrefetchScalarGridSpec(
            num_scalar_prefetch=1,
            grid=(n_steps,),
            in_specs=[pl.BlockSpec(memory_space=pl.ANY),
                      pl.BlockSpec((1, tl, d), lambda i, idx_ref: (i // nl, i % nl, 0)),
                      pl.BlockSpec((1, tl, LANES), lambda i, idx_ref: (i // nl, i % nl, 0)),
                      pl.BlockSpec((1, 1, d), lambda i, idx_ref: ((i // nl) * 6 + 5, 0, 0)),
                      full(lg), full(lb)],
            out_specs=pl.BlockSpec((1, tl, d), lambda i, idx_ref: (i // nl, i % nl, 0)),
            scratch_shapes=[pltpu.VMEM((TOP_K * tl * ns, lw), U32), pltpu.VMEM((TOP_K * tl * ns, lw), U32),
                            pltpu.SemaphoreType.DMA((2,))]),
        out_shape=jax.ShapeDtypeStruct((bsz, seq, d), out_dtype),
        compiler_params=_cp(("arbitrary",)),
        name="combine_ln",
    )(idx, ys, x1, top_w, mod3, lg, lb)


def _route(top_idx, n_experts):
    n_tok = top_idx.shape[0]
    n_assign = n_tok * TOP_K
    flat_e = top_idx.reshape(n_assign)
    iota = jnp.arange(n_assign, dtype=I32)
    _, order = lax.sort_key_val(flat_e, iota)
    _, inv_order = lax.sort_key_val(order, iota)
    experts = jnp.arange(n_experts, dtype=I32)
    onehot = flat_e[:, None] == experts[None, :]
    counts = jnp.sum(onehot, axis=0, dtype=I32)
    padded = (counts + EXPERT_BLOCK - 1) // EXPERT_BLOCK * EXPERT_BLOCK
    start = jnp.cumsum(counts) - counts
    pad_end = jnp.cumsum(padded)
    pad_start = pad_end - padded
    shift = jnp.sum(jnp.where(onehot, (pad_start - start)[None, :], 0), axis=1, dtype=I32)
    pos = (inv_order + shift).reshape(n_tok, TOP_K)
    n_blocks = -(-(n_assign + n_experts * (EXPERT_BLOCK - 1)) // EXPERT_BLOCK)
    block_start = jnp.arange(n_blocks, dtype=I32) * EXPERT_BLOCK
    block_e = jnp.minimum(jnp.sum(pad_end[None, :] <= block_start[:, None], axis=1, dtype=I32),
                          n_experts - 1)
    src_start = jnp.clip(block_start - (pad_start - start)[block_e], 0, n_assign)
    tok_sorted = jnp.concatenate([order // TOP_K, jnp.zeros((EXPERT_BLOCK,), I32)])
    row_src = src_start[:, None] + jnp.arange(EXPERT_BLOCK, dtype=I32)[None, :]
    row_tok = jnp.take(tok_sorted, row_src.reshape(n_blocks * EXPERT_BLOCK))
    return row_tok, block_e, pos


def kernel(x, c, ada_w, ada_b, w_in, shift_mu, rwkv_w0, rwkv_w2, rwkv_a0, rwkv_a2, rwkv_g2,
           rwkv_k_k, rwkv_k_a, rwkv_r_k, rwkv_ln_g, rwkv_ln_b, rwkv_proj,
           s5_a_re, s5_a_im, s5_log_dt, s5_b_re, s5_b_im, s5_c_re, s5_c_im, s5_d,
           s5_glu_w1, s5_glu_w2, w_out, ln1_g, ln1_b, ln2_g, ln2_b, router_w, router_b,
           exp_w_gate, exp_b_gate, exp_w_up, exp_b_up, exp_w_down, exp_b_down):
    out_dtype = x.dtype
    bsz, seq, d = x.shape
    depth = ada_w.shape[0]
    alpha = (2 * depth) ** 0.25
    dr = rwkv_w0.shape[1]
    ds = s5_glu_w1.shape[1]
    shift_w = shift_mu.shape[1]
    n_tok = bsz * seq
    n_experts = router_w.shape[2]
    x = x.astype(F32)
    c_pad = jnp.zeros((8, d), F32).at[:bsz].set(c.astype(F32))

    for l in range(depth):
        mod = _matmul(c_pad, ada_w[l], ada_b[l][None].astype(F32), out_dtype=F32, tm=8, tn=512,
                      name="adaln", silu_a=True)[:bsz]
        mod3 = mod.reshape(bsz * 6, 1, d)

        h = _ln_mod(x, mod3, 1, 0).reshape(n_tok, d)
        wi = w_in[l].astype(BF16)
        proj = lambda lo, hi, tn, name: _matmul(h, wi[:, lo:hi], out_dtype=BF16, tm=1024, tn=tn,
                                                name=name)
        p_rkv = proj(0, 3 * dr, 512, "proj_rkv").reshape(bsz, seq, 3 * dr)
        p_lora = proj(3 * dr, shift_w, 256, "proj_lora").reshape(bsz, seq, shift_w - 3 * dr)
        p_u = proj(shift_w, shift_w + ds, 512, "proj_s5").reshape(bsz, seq, ds)
        p_gates = proj(shift_w + ds, shift_w + ds + 2 * d, 512, "proj_gates")

        r, k, v, a, lw, g = _rwkv_prep(p_rkv, p_lora, shift_mu[l].astype(F32), rwkv_w0[l].astype(F32),
                                       rwkv_w2[l], rwkv_a0[l].astype(F32), rwkv_a2[l], rwkv_g2[l])
        yr = _rwkv_recurrence(r, k, v, a, lw, g, rwkv_k_k[l], rwkv_k_a[l], rwkv_r_k[l],
                              rwkv_ln_g[l], rwkv_ln_b[l])
        z = _s5_branch(p_u, s5_a_re[l], s5_a_im[l], s5_log_dt[l], s5_b_re[l], s5_b_im[l],
                       s5_c_re[l], s5_c_im[l], s5_d[l])
        merged = _merge(yr.reshape(n_tok, dr), z.reshape(n_tok, ds), rwkv_proj[l].astype(BF16),
                        s5_glu_w1[l].astype(BF16), s5_glu_w2[l].astype(BF16), p_gates)
        mix = _matmul(merged, w_out[l].astype(BF16), out_dtype=BF16, tm=1024, tn=512, name="w_out")

        x1, h2, top_idx, top_w = _resid_ln_router(x, mix.reshape(bsz, seq, d), mod3, ln1_g[l], ln1_b[l],
                                                  router_w[l], router_b[l], alpha=alpha)
        row_tok, block_e, pos = _route(top_idx.reshape(n_tok, LANES)[:, :TOP_K], n_experts)
        ns, lw = _pack_geometry(d)
        xs = _gather_rows_unpacked(h2.reshape(n_tok, ns, lw), row_tok, d)
        ys = _experts(xs, block_e, exp_w_gate[l], exp_b_gate[l], exp_w_up[l],
                      exp_b_up[l], exp_w_down[l], exp_b_down[l])
        x = _final(x1, ys.reshape(-1, ns, lw), pos, top_w, mod3, ln2_g[l], ln2_b[l], F32, alpha=alpha)
    return x.astype(out_dtype)
```

```python
import functools

import jax
import jax.numpy as jnp
from jax import lax
from jax.experimental import pallas as pl
from jax.experimental.pallas import tpu as pltpu

F32 = jnp.float32
BF16 = jnp.bfloat16
I32 = jnp.int32
U32 = jnp.uint32

LN_EPS = 1e-5
RWKV_GN_EPS = 64e-5
TOP_K = 4
EXPERT_BLOCK = 256
SWIGLU_LIMIT = 7.0
SWIGLU_ALPHA = 1.702
RWKV_CHUNK = 64
S5_CHUNK = 64
LANES = 128
VMEM_LIMIT = 56 * 1024 * 1024
HI = lax.Precision.HIGHEST


def _cp(sem, vmem=VMEM_LIMIT):
    return pltpu.CompilerParams(dimension_semantics=sem, vmem_limit_bytes=vmem)


def _dot(a, b):
    return jnp.dot(a.astype(BF16), b.astype(BF16), preferred_element_type=F32)


def _dot_nt(a, b, precision=None):
    return lax.dot_general(a, b, (((1,), (1,)), ((), ())), preferred_element_type=F32,
                           precision=precision)


def _dot_tn(a, b):
    return lax.dot_general(a.astype(BF16), b.astype(BF16), (((0,), (0,)), ((), ())),
                           preferred_element_type=F32)


def _ln(x):
    mean = jnp.mean(x, axis=-1, keepdims=True)
    xc = x - mean
    var = jnp.mean(xc * xc, axis=-1, keepdims=True)
    return xc * lax.rsqrt(var + LN_EPS)


def _sigmoid(x):
    return 1.0 / (1.0 + jnp.exp(-x))


def _tile(n, pref, align):
    if n <= pref:
        return n
    t = pref - pref % align
    while n % t:
        t -= align
    return t


def _pack_geometry(d):
    return d // (2 * LANES), LANES


def _pack_rows(x, o_ref):
    rows, d = x.shape
    half = d // 2
    ns, lw = o_ref.shape[0] // rows, o_ref.shape[1]
    lo = lax.bitcast_convert_type(x[:, :half].astype(BF16).astype(F32), U32) >> 16
    hi = lax.bitcast_convert_type(x[:, half:].astype(BF16).astype(F32), U32) & jnp.uint32(0xFFFF0000)
    words = hi | lo
    for s in range(ns):
        o_ref[pl.ds(s, rows, stride=ns), :] = words[:, s * lw:(s + 1) * lw]


def _unpack_rows(x_ref, rows):
    ns = x_ref.shape[0] // rows
    los, his = [], []
    for s in range(ns):
        words = x_ref[pl.ds(s, rows, stride=ns), :]
        los.append(lax.bitcast_convert_type(words << 16, F32))
        his.append(lax.bitcast_convert_type(words & jnp.uint32(0xFFFF0000), F32))
    return jnp.concatenate(los + his, axis=-1)


def _mm_body(a_ref, b_ref, bias_ref, o_ref, *, silu_a):
    a = a_ref[...]
    if silu_a:
        a = a * _sigmoid(a)
    o_ref[...] = (_dot(a, b_ref[...]) + bias_ref[...]).astype(o_ref.dtype)


def _matmul(a, b, bias=None, *, out_dtype, tm, tn, name, silu_a=False):
    m, k = a.shape
    n = b.shape[1]
    tm, tn = _tile(m, tm, 8), _tile(n, tn, LANES)
    if bias is None:
        bias = jnp.zeros((1, n), F32)
    return pl.pallas_call(
        functools.partial(_mm_body, silu_a=silu_a),
        grid=(m // tm, n // tn),
        in_specs=[pl.BlockSpec((tm, k), lambda i, j: (i, 0)),
                  pl.BlockSpec((k, tn), lambda i, j: (0, j)),
                  pl.BlockSpec((1, tn), lambda i, j: (0, j))],
        out_specs=pl.BlockSpec((tm, tn), lambda i, j: (i, j)),
        out_shape=jax.ShapeDtypeStruct((m, n), out_dtype),
        compiler_params=_cp(("parallel", "parallel")),
        name=name,
    )(a, b, bias)


def _ln_mod_body(x_ref, sc_ref, sh_ref, o_ref):
    y = _ln(x_ref[0])
    o_ref[0] = (y * (1.0 + sc_ref[0]) + sh_ref[0]).astype(o_ref.dtype)


def _ln_mod(x, mod3, sc_idx, sh_idx, *, tl=256):
    bsz, seq, d = x.shape
    tl = min(tl, seq)
    return pl.pallas_call(
        _ln_mod_body,
        grid=(bsz, seq // tl),
        in_specs=[pl.BlockSpec((1, tl, d), lambda b, l: (b, l, 0)),
                  pl.BlockSpec((1, 1, d), lambda b, l: (b * 6 + sc_idx, 0, 0)),
                  pl.BlockSpec((1, 1, d), lambda b, l: (b * 6 + sh_idx, 0, 0))],
        out_specs=pl.BlockSpec((1, tl, d), lambda b, l: (b, l, 0)),
        out_shape=jax.ShapeDtypeStruct((bsz, seq, d), BF16),
        compiler_params=_cp(("parallel", "parallel")),
        name="ln_modulate",
    )(x, mod3, mod3)


def _token_shift(x, carry_ref, mu):
    rolled = pltpu.roll(x, 1, 0)
    row = lax.broadcasted_iota(I32, x.shape, 0)
    prev = jnp.where(row == 0, carry_ref[...], rolled)
    carry_ref[...] = x[x.shape[0] - 1:, :]
    return x + (prev - x) * mu


def _rwkv_prep_body(prkv_ref, plora_ref, mu_rkv_ref, mu_lora_ref, w0_ref, w2_ref, a0_ref, a2_ref,
                    g2_ref, r_ref, k_ref, v_ref, a_ref, lw_ref, g_ref, carry_rkv, carry_lora,
                    *, dr, w_lora, a_lora):
    @pl.when(pl.program_id(1) == 0)
    def _():
        carry_rkv[...] = jnp.zeros_like(carry_rkv)
        carry_lora[...] = jnp.zeros_like(carry_lora)

    xs = _token_shift(prkv_ref[0].astype(F32), carry_rkv, mu_rkv_ref[...])
    r_ref[0] = xs[:, :dr].astype(r_ref.dtype)
    k_ref[0] = xs[:, dr:2 * dr].astype(k_ref.dtype)
    v_ref[0] = xs[:, 2 * dr:].astype(v_ref.dtype)

    ls = _token_shift(plora_ref[0].astype(F32), carry_lora, mu_lora_ref[...])
    xw = ls[:, :w_lora]
    xa = ls[:, w_lora:w_lora + a_lora]
    xg = ls[:, w_lora + a_lora:]
    z = -(w0_ref[...] + _dot(jnp.tanh(xw), w2_ref[...]))
    softplus = jnp.maximum(z, 0.0) + jnp.log1p(jnp.exp(-jnp.abs(z)))
    lw_ref[0] = -jnp.exp(-softplus - 0.5)
    a_ref[0] = _sigmoid(a0_ref[...] + _dot(xa, a2_ref[...])).astype(a_ref.dtype)
    g_ref[0] = _dot(_sigmoid(xg), g2_ref[...]).astype(g_ref.dtype)


def _rwkv_prep(p_rkv, p_lora, mu, w0, w2, a0, a2, g2, *, tl=256):
    bsz, seq, dr3 = p_rkv.shape
    dr = dr3 // 3
    lora = p_lora.shape[-1]
    w_lora, a_lora = w2.shape[0], a2.shape[0]
    tl = min(tl, seq)
    tok = lambda w: pl.BlockSpec((1, tl, w), lambda b, l: (b, l, 0))
    full = lambda a: pl.BlockSpec(a.shape, lambda b, l: (0,) * a.ndim)
    mu_rkv, mu_lora = mu[None, :dr3], mu[None, dr3:]
    args = (p_rkv, p_lora, mu_rkv, mu_lora, w0[None], w2.astype(BF16), a0[None], a2.astype(BF16),
            g2.astype(BF16))
    out = jax.ShapeDtypeStruct((bsz, seq, dr), BF16)
    return pl.pallas_call(
        functools.partial(_rwkv_prep_body, dr=dr, w_lora=w_lora, a_lora=a_lora),
        grid=(bsz, seq // tl),
        in_specs=[tok(dr3), tok(lora)] + [full(a) for a in args[2:]],
        out_specs=[tok(dr)] * 6,
        out_shape=[out, out, out, out, jax.ShapeDtypeStruct((bsz, seq, dr), F32), out],
        scratch_shapes=[pltpu.VMEM((1, dr3), F32), pltpu.VMEM((1, lora), F32)],
        compiler_params=_cp(("parallel", "arbitrary")),
        name="rwkv_prep",
    )(*args)


def _split_bf16(x):
    hi = x.astype(BF16)
    return hi, (x - hi.astype(F32)).astype(BF16)


def _head_sum(x, bd_ref):
    c, w = x.shape
    gw = bd_ref.shape[0]
    ng = w // gw
    hi, lo = _split_bf16(x)
    groups = [p[:, j * gw:(j + 1) * gw] for p in (hi, lo) for j in range(ng)]
    res = jnp.dot(jnp.concatenate(groups, axis=0), bd_ref[...], preferred_element_type=F32)
    parts = [res[j * c:(j + 1) * c] + res[(ng + j) * c:(ng + j + 1) * c] for j in range(ng)]
    return jnp.concatenate(parts, axis=-1)


def _rwkv_rec_body(r_ref, k_ref, v_ref, a_ref, lw_ref, g_ref, kk_ref, ka_ref, rk_ref, lng_ref,
                   lnb_ref, bd_ref, y_ref, state, ybuf, *, heads, n):
    c = r_ref.shape[1]

    @pl.when(pl.program_id(2) == 0)
    def _():
        state[...] = jnp.zeros_like(state)

    row = lax.broadcasted_iota(I32, (c, c), 0)
    col = lax.broadcasted_iota(I32, (c, c), 1)
    strict = row > col
    eye = (row == col).astype(F32)
    tril_b = (row >= col).astype(BF16)
    row2 = lax.broadcasted_iota(I32, (c, 2 * c), 0)
    col2 = lax.broadcasted_iota(I32, (c, 2 * c), 1)
    strict_right = (col2 >= c) & (row2 > col2 - c)
    lower_pair = row2 >= jnp.where(col2 >= c, col2 - c, col2)

    r = r_ref[0].astype(F32)
    k = k_ref[0].astype(F32)
    v = v_ref[0].astype(F32)
    a = a_ref[0].astype(F32)
    lw = lw_ref[0]
    lw_hi, lw_lo = _split_bf16(lw)
    cl = (jnp.dot(tril_b, lw_hi, preferred_element_type=F32)
          + jnp.dot(tril_b, lw_lo, preferred_element_type=F32))
    w_inc = jnp.exp(cl)
    w_inv = jnp.exp(-cl)
    w_prev = jnp.exp(cl - lw)
    kk = k * kk_ref[...]
    kk = kk / jnp.maximum(jnp.sqrt(_head_sum(kk * kk, bd_ref)), 1e-12)
    k2 = k * (1.0 + (a - 1.0) * ka_ref[...])
    rt = (r * w_inc).astype(BF16)
    kkt = (kk * w_prev).astype(BF16)
    bt = (kk * a * w_inv).astype(BF16)
    kt = (k2 * w_inv).astype(BF16)
    vb = v.astype(BF16)
    bonus = _head_sum(r * k2 * rk_ref[...], bd_ref) * v

    hs = range(heads)
    sls = [slice(h * n, (h + 1) * n) for h in hs]
    lhs2 = [jnp.concatenate([kkt[:, sl], rt[:, sl]], axis=0) for sl in sls]
    rhs2 = [jnp.concatenate([bt[:, sl], kt[:, sl]], axis=0) for sl in sls]
    vh = [vb[:, sl] for sl in sls]
    s0 = [state[h] for h in hs]
    gram = [_dot_nt(lhs2[h], rhs2[h]) for h in hs]
    m1 = [_dot_nt(lhs2[h], s0[h].astype(BF16)) for h in hs]
    l_b = [jnp.where(strict, gram[h][:c, :c], 0.0) for h in hs]
    lk_only = [jnp.where(strict_right, gram[h][:c], 0.0).astype(BF16) for h in hs]
    p_bk = [jnp.where(lower_pair, gram[h][c:], 0.0).astype(BF16) for h in hs]
    rhs_u = [m1[h][:c] + jnp.dot(lk_only[h], jnp.concatenate([vh[h], vh[h]], axis=0),
                                 preferred_element_type=F32) for h in hs]

    t_inv = [eye - l_b[h] for h in hs]
    pw = [_dot(l_b[h], l_b[h]) for h in hs]
    span = 4
    while span < c:
        both = [_dot(jnp.concatenate([t_inv[h], pw[h]], axis=0), pw[h]) for h in hs]
        t_inv = [t_inv[h] + both[h][:c] for h in hs]
        pw = [both[h][c:] for h in hs]
        span *= 2
    t_inv = [t_inv[h] + _dot(t_inv[h], pw[h]) for h in hs]

    u = [-_dot(t_inv[h], rhs_u[h]) for h in hs]
    uv = [jnp.concatenate([u[h].astype(BF16), vh[h]], axis=0) for h in hs]
    y_h = [m1[h][c:] + jnp.dot(p_bk[h], uv[h], preferred_element_type=F32) for h in hs]
    s_new = [s0[h] + _dot_tn(uv[h], rhs2[h]) for h in hs]
    for h in hs:
        ybuf[:, sls[h]] = y_h[h]
        state[h] = s_new[h] * w_inc[c - 1:, sls[h]]

    y = ybuf[...]
    inv_n = 1.0 / n
    yc = y - _head_sum(y, bd_ref) * inv_n
    yv = _head_sum(yc * yc, bd_ref) * inv_n
    yn = yc * lax.rsqrt(yv + RWKV_GN_EPS) * lng_ref[...] + lnb_ref[...]
    y_ref[0] = ((yn + bonus) * g_ref[0].astype(F32)).astype(y_ref.dtype)


def _rwkv_recurrence(r, k, v, a, lw, g, k_k, k_a, r_k, ln_g, ln_b, *, heads_per_step=32):
    bsz, seq, dr = r.shape
    n = k_k.shape[-1]
    hb = min(heads_per_step, dr // n)
    wblk = hb * n
    gw = min(2 * LANES, wblk)
    assert wblk % LANES == 0 and dr % wblk == 0 and seq % RWKV_CHUNK == 0 and wblk % gw == 0
    c = RWKV_CHUNK
    tok = pl.BlockSpec((1, c, wblk), lambda b, h, t: (b, t, h))
    par = pl.BlockSpec((1, wblk), lambda b, h, t: (0, h))
    flat = lambda p: p.reshape(1, dr).astype(F32)
    gi = lax.broadcasted_iota(I32, (gw, gw), 0) // n
    gj = lax.broadcasted_iota(I32, (gw, gw), 1) // n
    block_diag = (gi == gj).astype(BF16)
    return pl.pallas_call(
        functools.partial(_rwkv_rec_body, heads=hb, n=n),
        grid=(bsz, dr // wblk, seq // c),
        in_specs=[tok] * 6 + [par] * 5 + [pl.BlockSpec((gw, gw), lambda b, h, t: (0, 0))],
        out_specs=tok,
        out_shape=jax.ShapeDtypeStruct((bsz, seq, dr), BF16),
        scratch_shapes=[pltpu.VMEM((hb, n, n), F32), pltpu.VMEM((c, wblk), F32)],
        compiler_params=_cp(("parallel", "parallel", "arbitrary")),
        name="rwkv_recurrence",
    )(r, k, v, a, lw, g, flat(k_k), flat(k_a), flat(r_k), flat(ln_g), flat(ln_b), block_diag)


def _s5_param_body(are_ref, aim_ref, ldt_ref, bre_ref, bim_ref, cre_ref, cim_ref, d_ref,
                   t_ref, bmat_ref, cmat_ref, q_ref, *, lc, n_levels):
    a_re, a_im = are_ref[0], aim_ref[0]
    dt = jnp.exp(ldt_ref[0])
    lam = a_re * dt
    ang = a_im * dt
    mag = jnp.exp(lam)
    ab_re, ab_im = mag * jnp.cos(ang), mag * jnp.sin(ang)
    den = a_re * a_re + a_im * a_im
    f_re = ((ab_re - 1.0) * a_re + ab_im * a_im) / den
    f_im = (ab_im * a_re - (ab_re - 1.0) * a_im) / den
    b_re, b_im = bre_ref[0], bim_ref[0]
    bb_re = f_re * b_re - f_im * b_im
    bb_im = f_re * b_im + f_im * b_re
    c_re, c_im = cre_ref[0], cim_ref[0]
    nh, p2 = b_re.shape
    lo = lax.broadcasted_iota(I32, (1, p2), 1) < (p2 // 2)

    def power(tf):
        m = jnp.exp(tf * lam)
        return m * jnp.cos(tf * ang), m * jnp.sin(tf * ang)

    def cmul3(xr, xi, pr, pi):
        rr = xr[None] * pr[:, None, :] - xi[None] * pi[:, None, :]
        ri = xr[None] * pi[:, None, :] + xi[None] * pr[:, None, :]
        return rr.reshape(lc * nh, p2), ri.reshape(lc * nh, p2)

    tcol = lax.broadcasted_iota(I32, (lc, 1), 0).astype(F32)
    nr, ni = power(-tcol)
    pr, pi = power(tcol)
    bm_re, bm_im = cmul3(bb_re, bb_im, nr, ni)
    cm_re, cm_im = cmul3(c_re, c_im, pr, pi)
    x = jnp.where(lo, bm_re, -bm_im)
    yc = jnp.where(lo, cm_re, cm_im)
    x_hi, x_lo = _split_bf16(x)
    y_hi, y_lo = _split_bf16(yc)
    t = _dot_nt(x_hi, y_hi) + _dot_nt(x_hi, y_lo) + _dot_nt(x_lo, y_hi)
    m = lc * nh
    ri = lax.broadcasted_iota(I32, (m, m), 0)
    ci = lax.broadcasted_iota(I32, (m, m), 1)
    t = jnp.where(ci // nh >= ri // nh, t, 0.0)
    t = jnp.where(ri == ci, t + d_ref[0], t)
    t_ref[0] = t.astype(t_ref.dtype)

    er, ei = power(jnp.full((1, 1), lc - 1.0, F32))
    bmat_ref[0] = jnp.where(lo, bm_re * er - bm_im * ei, bm_re * ei + bm_im * er).astype(bmat_ref.dtype)
    cmat_ref[0] = jnp.where(lo, cm_re * ab_re - cm_im * ab_im,
                            -(cm_re * ab_im + cm_im * ab_re)).astype(cmat_ref.dtype)
    for i in range(n_levels):
        qr, qi = power(jnp.full((1, 1), float(lc * 2 ** i), F32))
        q_ref[0, i:i + 1, :] = qr
        q_ref[0, n_levels + i:n_levels + i + 1, :] = jnp.where(lo, -qi, qi)


def _s5_conv_body(u_ref, t_ref, bmat_ref, cmat_ref, q_ref, z_ref, *, nc, n_levels):
    u = u_ref[0]
    zloc = jnp.dot(u, bmat_ref[0], preferred_element_type=F32)
    rows, p2 = zloc.shape
    n_in_seq = lax.broadcasted_iota(I32, (rows, p2), 0) % nc
    s = jnp.where(n_in_seq >= 1, pltpu.roll(zloc, 1, 0), 0.0)
    for i in range(n_levels):
        step = 2 ** i
        sh = jnp.where(n_in_seq >= step, pltpu.roll(s, step, 0), 0.0)
        sw = pltpu.roll(sh, p2 // 2, 1)
        s = s + q_ref[0, i:i + 1, :] * sh + q_ref[0, n_levels + i:n_levels + i + 1, :] * sw
    y = jnp.dot(u, t_ref[0], preferred_element_type=F32) + _dot_nt(s.astype(BF16), cmat_ref[0])
    z_ref[0] = jax.nn.gelu(y).astype(z_ref.dtype)


def _s5_branch(p_u, a_re, a_im, log_dt, b_re, b_im, c_re, c_im, d):
    bsz, seq, ds = p_u.shape
    g, p = a_re.shape
    nh = ds // g
    lc = min(S5_CHUNK, seq)
    nc = seq // lc
    n_levels = max(1, (nc - 1).bit_length())
    rows = bsz * nc
    m = lc * nh
    tile2 = lambda a: jnp.concatenate([a, a], axis=-1).astype(F32)
    args = (tile2(a_re)[:, None, :], tile2(a_im)[:, None, :], log_dt.reshape(g, 1, 1).astype(F32),
            tile2(jnp.swapaxes(b_re, 1, 2)), tile2(jnp.swapaxes(b_im, 1, 2)),
            tile2(c_re), tile2(c_im), jnp.tile(d.astype(F32), (1, lc))[:, None, :])
    gspec = lambda a: pl.BlockSpec((1,) + a.shape[1:], lambda i: (i, 0, 0))
    qrows = 2 * n_levels
    t_mat, bmat, cmat, q = pl.pallas_call(
        functools.partial(_s5_param_body, lc=lc, n_levels=n_levels),
        grid=(g,),
        in_specs=[gspec(a) for a in args],
        out_specs=[pl.BlockSpec((1, m, m), lambda i: (i, 0, 0)),
                   pl.BlockSpec((1, m, 2 * p), lambda i: (i, 0, 0)),
                   pl.BlockSpec((1, m, 2 * p), lambda i: (i, 0, 0)),
                   pl.BlockSpec((1, qrows, 2 * p), lambda i: (i, 0, 0))],
        out_shape=[jax.ShapeDtypeStruct((g, m, m), BF16),
                   jax.ShapeDtypeStruct((g, m, 2 * p), BF16),
                   jax.ShapeDtypeStruct((g, m, 2 * p), BF16),
                   jax.ShapeDtypeStruct((g, qrows, 2 * p), F32)],
        compiler_params=_cp(("parallel",)),
        name="s5_params",
    )(*args)

    u = p_u.reshape(bsz, nc, lc, g, nh).transpose(3, 0, 1, 2, 4).reshape(g, rows, m)
    z = pl.pallas_call(
        functools.partial(_s5_conv_body, nc=nc, n_levels=n_levels),
        grid=(g,),
        in_specs=[gspec(u), gspec(t_mat), gspec(bmat), gspec(cmat), gspec(q)],
        out_specs=pl.BlockSpec((1, rows, m), lambda i: (i, 0, 0)),
        out_shape=jax.ShapeDtypeStruct((g, rows, m), BF16),
        compiler_params=_cp(("parallel",)),
        name="s5_conv",
    )(u, t_mat, bmat, cmat, q)
    return z.reshape(g, bsz, nc, lc, nh).transpose(1, 2, 3, 0, 4).reshape(bsz, seq, ds)


def _merge_body(yr_ref, z_ref, wp_ref, w1_ref, w2_ref, gr_ref, gs_ref, o_ref):
    y_rwkv = jnp.dot(yr_ref[...], wp_ref[...], preferred_element_type=F32)
    z = z_ref[...]
    y_s5 = (jnp.dot(z, w1_ref[...], preferred_element_type=F32)
            * _sigmoid(jnp.dot(z, w2_ref[...], preferred_element_type=F32)))
    o_ref[...] = (_sigmoid(gr_ref[...].astype(F32)) * y_rwkv
                  + _sigmoid(gs_ref[...].astype(F32)) * y_s5).astype(o_ref.dtype)


def _merge(yr, z, wp, w1, w2, p_gates, *, tm=512, tn=512):
    t, kr = yr.shape
    ks = z.shape[1]
    d = wp.shape[1]
    tm, tn = min(tm, t), min(tn, d)
    nj = d // tn
    return pl.pallas_call(
        _merge_body,
        grid=(t // tm, nj),
        in_specs=[pl.BlockSpec((tm, kr), lambda i, j: (i, 0)),
                  pl.BlockSpec((tm, ks), lambda i, j: (i, 0)),
                  pl.BlockSpec((kr, tn), lambda i, j: (0, j)),
                  pl.BlockSpec((ks, tn), lambda i, j: (0, j)),
                  pl.BlockSpec((ks, tn), lambda i, j: (0, j)),
                  pl.BlockSpec((tm, tn), lambda i, j: (i, j)),
                  pl.BlockSpec((tm, tn), lambda i, j: (i, j + nj))],
        out_specs=pl.BlockSpec((tm, tn), lambda i, j: (i, j)),
        out_shape=jax.ShapeDtypeStruct((t, d), BF16),
        compiler_params=_cp(("parallel", "parallel")),
        name="merge_branches",
    )(yr, z, wp, w1, w2, p_gates, p_gates)


def _router_body(x_ref, mix_ref, g1_ref, lng_ref, lnb_ref, sc_ref, sh_ref, rw_ref, rb_ref,
                 x1_ref, h2_ref, idx_ref, w_ref, *, alpha):
    x1 = _ln(alpha * x_ref[0] + g1_ref[0] * mix_ref[0].astype(F32)) * lng_ref[...] + lnb_ref[...]
    x1_ref[0] = x1
    h2 = _ln(x1) * (1.0 + sc_ref[0]) + sh_ref[0]
    _pack_rows(h2, h2_ref.at[0])
    logits = jnp.dot(h2, rw_ref[...], preferred_element_type=F32, precision=HI) + rb_ref[...]
    lane = lax.broadcasted_iota(I32, logits.shape, 1)
    lane_f = lane.astype(F32)
    idx_out = jnp.zeros(logits.shape, I32)
    val_out = jnp.zeros(logits.shape, F32)
    vals = []
    for kth in range(TOP_K):
        mx = jnp.max(logits, axis=-1, keepdims=True)
        sel = jnp.min(jnp.where(logits == mx, lane_f, float(LANES)), axis=-1, keepdims=True).astype(I32)
        vals.append(mx)
        idx_out = jnp.where(lane == kth, sel, idx_out)
        logits = jnp.where(lane == sel, -jnp.inf, logits)
    exps = [jnp.exp(vv - vals[0]) for vv in vals]
    tot = exps[0]
    for e in exps[1:]:
        tot = tot + e
    for kth in range(TOP_K):
        val_out = jnp.where(lane == kth, exps[kth] / tot, val_out)
    idx_ref[0] = idx_out
    w_ref[0] = val_out


def _resid_ln_router(x, mix, mod3, ln_g, ln_b, router_w, router_b, *, alpha, tl=256):
    bsz, seq, d = x.shape
    e = router_w.shape[1]
    tl = min(tl, seq)
    ns, lw = _pack_geometry(d)
    rw = jnp.zeros((d, LANES), F32).at[:, :e].set(router_w.astype(F32))
    rb = jnp.full((1, LANES), -jnp.inf, F32).at[0, :e].set(router_b.astype(F32))
    tok = lambda w: pl.BlockSpec((1, tl, w), lambda b, l: (b, l, 0))
    modspec = lambda j: pl.BlockSpec((1, 1, d), lambda b, l: (b * 6 + j, 0, 0))
    full = lambda a: pl.BlockSpec(a.shape, lambda b, l: (0,) * a.ndim)
    args = (x, mix, mod3, ln_g[None].astype(F32), ln_b[None].astype(F32), mod3, mod3, rw, rb)
    return pl.pallas_call(
        functools.partial(_router_body, alpha=alpha),
        grid=(bsz, seq // tl),
        in_specs=[tok(d), tok(d), modspec(2), full(args[3]), full(args[4]), modspec(4), modspec(3),
                  full(rw), full(rb)],
        out_specs=[tok(d), pl.BlockSpec((1, tl * ns, lw), lambda b, l: (b, l, 0)),
                   tok(LANES), tok(LANES)],
        out_shape=[jax.ShapeDtypeStruct((bsz, seq, d), F32),
                   jax.ShapeDtypeStruct((bsz, seq * ns, lw), U32),
                   jax.ShapeDtypeStruct((bsz, seq, LANES), I32),
                   jax.ShapeDtypeStruct((bsz, seq, LANES), F32)],
        compiler_params=_cp(("parallel", "parallel")),
        name="resid_ln_router",
    )(*args)


def _gather_unpack_body(idx_ref, src_ref, o_ref, buf0, buf1, stage, gather_sem, write_sem, *,
                        rows_per_step, n_steps, ns):
    bufs = (buf0, buf1)

    def row_copy(step, slot, r):
        dst = bufs[slot].at[pl.ds(pl.multiple_of(r * ns, ns), ns)]
        return pltpu.make_async_copy(src_ref.at[idx_ref[step * rows_per_step + r]], dst,
                                     gather_sem.at[slot])

    def start_gather(step, slot):
        def body(r8, carry):
            for u in range(8):
                row_copy(step, slot, r8 * 8 + u).start(priority=u % 2)
            return carry
        lax.fori_loop(0, rows_per_step // 8, body, 0)

    def wait_gather(step, slot):
        def body(r, carry):
            row_copy(step, slot, r).wait()
            return carry
        lax.fori_loop(0, rows_per_step, body, 0, unroll=8)

    def write_back(step, slot):
        return pltpu.make_async_copy(stage.at[slot], o_ref.at[pl.ds(step * rows_per_step, rows_per_step)],
                                     write_sem.at[slot])

    i = pl.program_id(0)

    @pl.when(i == 0)
    def _():
        start_gather(0, 0)

    for slot in (0, 1):
        other = 1 - slot

        @pl.when(i % 2 == slot)
        def _():
            @pl.when(i >= 1)
            def _():
                write_back(i - 1, other).wait()

            @pl.when(i + 1 < n_steps)
            def _():
                start_gather(i + 1, other)

            wait_gather(i, slot)
            stage[slot] = _unpack_rows(bufs[slot], rows_per_step).astype(BF16)
            write_back(i, slot).start()

            @pl.when(i == n_steps - 1)
            def _():
                write_back(i, slot).wait()


def _gather_rows_unpacked(src, idx, d, *, rows_per_step=256):
    n = idx.shape[0]
    ns = src.shape[1]
    rows_per_step = min(rows_per_step, n)
    assert n % rows_per_step == 0 and src.shape[2] == LANES and d == 2 * ns * LANES
    return pl.pallas_call(
        functools.partial(_gather_unpack_body, rows_per_step=rows_per_step,
                          n_steps=n // rows_per_step, ns=ns),
        grid_spec=pltpu.PrefetchScalarGridSpec(
            num_scalar_prefetch=1,
            grid=(n // rows_per_step,),
            in_specs=[pl.BlockSpec(memory_space=pl.ANY)],
            out_specs=pl.BlockSpec(memory_space=pl.ANY),
            scratch_shapes=[pltpu.VMEM((rows_per_step * ns, LANES), src.dtype),
                            pltpu.VMEM((rows_per_step * ns, LANES), src.dtype),
                            pltpu.VMEM((2, rows_per_step, d), BF16),
                            pltpu.SemaphoreType.DMA((2,)), pltpu.SemaphoreType.DMA((2,))],
        ),
        out_shape=jax.ShapeDtypeStruct((n, d), BF16),
        compiler_params=_cp(("arbitrary",)),
        name="gather_rows_unpack",
    )(idx, src)


def _expert_changed(be_ref, i):
    return jnp.logical_or(i == 0, be_ref[i] != be_ref[jnp.maximum(i - 1, 0)])


def _expert_up_body(be_ref, x_ref, wg_ref, bg_ref, wu_ref, bu_ref, o_ref, wg_s, wu_s):
    @pl.when(_expert_changed(be_ref, pl.program_id(1)))
    def _():
        wg_s[...] = wg_ref[0].astype(BF16)
        wu_s[...] = wu_ref[0].astype(BF16)

    x = x_ref[...]
    gate = jnp.dot(x, wg_s[...], preferred_element_type=F32) + bg_ref[0]
    up = jnp.dot(x, wu_s[...], preferred_element_type=F32) + bu_ref[0]
    gate = jnp.minimum(gate, SWIGLU_LIMIT)
    up = jnp.clip(up, -SWIGLU_LIMIT, SWIGLU_LIMIT)
    o_ref[...] = ((up + 1.0) * gate * _sigmoid(SWIGLU_ALPHA * gate)).astype(o_ref.dtype)


def _expert_down_body(be_ref, a_ref, wd_ref, bd_ref, o_ref, wd_s):
    @pl.when(_expert_changed(be_ref, pl.program_id(0)))
    def _():
        wd_s[...] = wd_ref[0].astype(BF16)

    _pack_rows(jnp.dot(a_ref[...], wd_s[...], preferred_element_type=F32) + bd_ref[0], o_ref)


def _experts(xs, block_e, w_gate, b_gate, w_up, b_up, w_down, b_down, *, tf=512):
    e, d, f = w_gate.shape
    ns, lw = _pack_geometry(d)
    n_rows = xs.shape[0]
    nb = n_rows // EXPERT_BLOCK
    tf = _tile(f, tf, LANES)
    packed_rows = lambda imap: pl.BlockSpec((EXPERT_BLOCK * ns, lw), imap)
    act = pl.pallas_call(
        _expert_up_body,
        grid_spec=pltpu.PrefetchScalarGridSpec(
            num_scalar_prefetch=1, grid=(f // tf, nb),
            in_specs=[pl.BlockSpec((EXPERT_BLOCK, d), lambda j, i, be: (i, 0)),
                      pl.BlockSpec((1, d, tf), lambda j, i, be: (be[i], 0, j)),
                      pl.BlockSpec((1, 1, tf), lambda j, i, be: (be[i], 0, j)),
                      pl.BlockSpec((1, d, tf), lambda j, i, be: (be[i], 0, j)),
                      pl.BlockSpec((1, 1, tf), lambda j, i, be: (be[i], 0, j))],
            out_specs=pl.BlockSpec((EXPERT_BLOCK, tf), lambda j, i, be: (i, j)),
            scratch_shapes=[pltpu.VMEM((d, tf), BF16), pltpu.VMEM((d, tf), BF16)]),
        out_shape=jax.ShapeDtypeStruct((n_rows, f), BF16),
        compiler_params=_cp(("arbitrary", "arbitrary")),
        name="expert_up",
    )(block_e, xs, w_gate, b_gate[:, None, :].astype(F32), w_up, b_up[:, None, :].astype(F32))
    return pl.pallas_call(
        _expert_down_body,
        grid_spec=pltpu.PrefetchScalarGridSpec(
            num_scalar_prefetch=1, grid=(nb,),
            in_specs=[pl.BlockSpec((EXPERT_BLOCK, f), lambda i, be: (i, 0)),
                      pl.BlockSpec((1, f, d), lambda i, be: (be[i], 0, 0)),
                      pl.BlockSpec((1, 1, d), lambda i, be: (be[i], 0, 0))],
            out_specs=packed_rows(lambda i, be: (i, 0)),
            scratch_shapes=[pltpu.VMEM((f, d), BF16)]),
        out_shape=jax.ShapeDtypeStruct((n_rows * ns, lw), U32),
        compiler_params=_cp(("arbitrary",)),
        name="expert_down",
    )(block_e, act, w_down, b_down[:, None, :].astype(F32))


def _final_body(idx_ref, ys_ref, x1_ref, w_ref, g2_ref, lng_ref, lnb_ref, o_ref, buf0, buf1,
                gather_sem, *, alpha, tl, ns, n_steps):
    bufs = (buf0, buf1)
    rows = TOP_K * tl

    def row_copy(step, slot, r):
        dst = bufs[slot].at[pl.ds(pl.multiple_of(r * ns, ns), ns)]
        return pltpu.make_async_copy(ys_ref.at[idx_ref[step * rows + r]], dst, gather_sem.at[slot])

    def start_gather(step, slot):
        def body(r8, carry):
            for u in range(8):
                row_copy(step, slot, r8 * 8 + u).start(priority=u % 2)
            return carry
        lax.fori_loop(0, rows // 8, body, 0)

    def wait_gather(step, slot):
        def body(r, carry):
            row_copy(step, slot, r).wait()
            return carry
        lax.fori_loop(0, rows, body, 0, unroll=8)

    i = pl.program_id(0)

    @pl.when(i == 0)
    def _():
        start_gather(0, 0)

    for slot in (0, 1):
        @pl.when(i % 2 == slot)
        def _():
            @pl.when(i + 1 < n_steps)
            def _():
                start_gather(i + 1, 1 - slot)

            wait_gather(i, slot)
            wts = w_ref[0]
            ff = None
            for kth in range(TOP_K):
                y_k = _unpack_rows(bufs[slot].at[pl.ds(kth * tl * ns, tl * ns)], tl)
                term = wts[:, kth:kth + 1] * y_k
                ff = term if ff is None else ff + term
            o_ref[0] = (_ln(alpha * x1_ref[0] + g2_ref[0] * ff) * lng_ref[...]
                        + lnb_ref[...]).astype(o_ref.dtype)


def _final(x1, ys, pos, top_w, mod3, ln_g, ln_b, out_dtype, *, alpha, tl=256):
    bsz, seq, d = x1.shape
    ns, lw = _pack_geometry(d)
    tl = min(tl, seq)
    nl = seq // tl
    n_steps = bsz * nl
    idx = pos.reshape(n_steps, tl, TOP_K).transpose(0, 2, 1).reshape(-1)
    full = lambda a: pl.BlockSpec(a.shape, lambda i, idx_ref: (0,) * a.ndim)
    lg, lb = ln_g[None].astype(F32), ln_b[None].astype(F32)
    return pl.pallas_call(
        functools.partial(_final_body, alpha=alpha, tl=tl, ns=ns, n_steps=n_steps),
        grid_spec=pltpu.PrefetchScalarGridSpec(
            num_scalar_prefetch=1,
            grid=(n_steps,),
            in_specs=[pl.BlockSpec(memory_space=pl.ANY),
                      pl.BlockSpec((1, tl, d), lambda i, idx_ref: (i // nl, i % nl, 0)),
                      pl.BlockSpec((1, tl, LANES), lambda i, idx_ref: (i // nl, i % nl, 0)),
                      pl.BlockSpec((1, 1, d), lambda i, idx_ref: ((i // nl) * 6 + 5, 0, 0)),
                      full(lg), full(lb)],
            out_specs=pl.BlockSpec((1, tl, d), lambda i, idx_ref: (i // nl, i % nl, 0)),
            scratch_shapes=[pltpu.VMEM((TOP_K * tl * ns, lw), U32), pltpu.VMEM((TOP_K * tl * ns, lw), U32),
                            pltpu.SemaphoreType.DMA((2,))]),
        out_shape=jax.ShapeDtypeStruct((bsz, seq, d), out_dtype),
        compiler_params=_cp(("arbitrary",)),
        name="combine_ln",
    )(idx, ys, x1, top_w, mod3, lg, lb)


def _route(top_idx, n_experts):
    n_tok = top_idx.shape[0]
    n_assign = n_tok * TOP_K
    flat_e = top_idx.reshape(n_assign)
    iota = jnp.arange(n_assign, dtype=I32)
    _, order = lax.sort_key_val(flat_e, iota)
    _, inv_order = lax.sort_key_val(order, iota)
    experts = jnp.arange(n_experts, dtype=I32)
    onehot = flat_e[:, None] == experts[None, :]
    counts = jnp.sum(onehot, axis=0, dtype=I32)
    padded = (counts + EXPERT_BLOCK - 1) // EXPERT_BLOCK * EXPERT_BLOCK
    start = jnp.cumsum(counts) - counts
    pad_end = jnp.cumsum(padded)
    pad_start = pad_end - padded
    shift = jnp.sum(jnp.where(onehot, (pad_start - start)[None, :], 0), axis=1, dtype=I32)
    pos = (inv_order + shift).reshape(n_tok, TOP_K)
    n_blocks = -(-(n_assign + n_experts * (EXPERT_BLOCK - 1)) // EXPERT_BLOCK)
    block_start = jnp.arange(n_blocks, dtype=I32) * EXPERT_BLOCK
    block_e = jnp.minimum(jnp.sum(pad_end[None, :] <= block_start[:, None], axis=1, dtype=I32),
                          n_experts - 1)
    src_start = jnp.clip(block_start - (pad_start - start)[block_e], 0, n_assign)
    tok_sorted = jnp.concatenate([order // TOP_K, jnp.zeros((EXPERT_BLOCK,), I32)])
    row_src = src_start[:, None] + jnp.arange(EXPERT_BLOCK, dtype=I32)[None, :]
    row_tok = jnp.take(tok_sorted, row_src.reshape(n_blocks * EXPERT_BLOCK))
    return row_tok, block_e, pos


def kernel(x, c, ada_w, ada_b, w_in, shift_mu, rwkv_w0, rwkv_w2, rwkv_a0, rwkv_a2, rwkv_g2,
           rwkv_k_k, rwkv_k_a, rwkv_r_k, rwkv_ln_g, rwkv_ln_b, rwkv_proj,
           s5_a_re, s5_a_im, s5_log_dt, s5_b_re, s5_b_im, s5_c_re, s5_c_im, s5_d,
           s5_glu_w1, s5_glu_w2, w_out, ln1_g, ln1_b, ln2_g, ln2_b, router_w, router_b,
           exp_w_gate, exp_b_gate, exp_w_up, exp_b_up, exp_w_down, exp_b_down):
    out_dtype = x.dtype
    bsz, seq, d = x.shape
    depth = ada_w.shape[0]
    alpha = (2 * depth) ** 0.25
    dr = rwkv_w0.shape[1]
    ds = s5_glu_w1.shape[1]
    shift_w = shift_mu.shape[1]
    n_tok = bsz * seq
    n_experts = router_w.shape[2]
    x = x.astype(F32)
    c_pad = jnp.zeros((8, d), F32).at[:bsz].set(c.astype(F32))

    for l in range(depth):
        mod = _matmul(c_pad, ada_w[l], ada_b[l][None].astype(F32), out_dtype=F32, tm=8, tn=512,
                      name="adaln", silu_a=True)[:bsz]
        mod3 = mod.reshape(bsz * 6, 1, d)

        h = _ln_mod(x, mod3, 1, 0).reshape(n_tok, d)
        wi = w_in[l].astype(BF16)
        proj = lambda lo, hi, tn, name: _matmul(h, wi[:, lo:hi], out_dtype=BF16, tm=1024, tn=tn,
                                                name=name)
        p_rkv = proj(0, 3 * dr, 512, "proj_rkv").reshape(bsz, seq, 3 * dr)
        p_lora = proj(3 * dr, shift_w, 256, "proj_lora").reshape(bsz, seq, shift_w - 3 * dr)
        p_u = proj(shift_w, shift_w + ds, 512, "proj_s5").reshape(bsz, seq, ds)
        p_gates = proj(shift_w + ds, shift_w + ds + 2 * d, 512, "proj_gates")

        r, k, v, a, lw, g = _rwkv_prep(p_rkv, p_lora, shift_mu[l].astype(F32), rwkv_w0[l].astype(F32),
                                       rwkv_w2[l], rwkv_a0[l].astype(F32), rwkv_a2[l], rwkv_g2[l])
        yr = _rwkv_recurrence(r, k, v, a, lw, g, rwkv_k_k[l], rwkv_k_a[l], rwkv_r_k[l],
                              rwkv_ln_g[l], rwkv_ln_b[l])
        z = _s5_branch(p_u, s5_a_re[l], s5_a_im[l], s5_log_dt[l], s5_b_re[l], s5_b_im[l],
                       s5_c_re[l], s5_c_im[l], s5_d[l])
        merged = _merge(yr.reshape(n_tok, dr), z.reshape(n_tok, ds), rwkv_proj[l].astype(BF16),
                        s5_glu_w1[l].astype(BF16), s5_glu_w2[l].astype(BF16), p_gates)
        mix = _matmul(merged, w_out[l].astype(BF16), out_dtype=BF16, tm=1024, tn=512, name="w_out")

        x1, h2, top_idx, top_w = _resid_ln_router(x, mix.reshape(bsz, seq, d), mod3, ln1_g[l], ln1_b[l],
                                                  router_w[l], router_b[l], alpha=alpha)
        row_tok, block_e, pos = _route(top_idx.reshape(n_tok, LANES)[:, :TOP_K], n_experts)
        ns, lw = _pack_geometry(d)
        xs = _gather_rows_unpacked(h2.reshape(n_tok, ns, lw), row_tok, d)
        ys = _experts(xs, block_e, exp_w_gate[l], exp_b_gate[l], exp_w_up[l],
                      exp_b_up[l], exp_w_down[l], exp_b_down[l])
        x = _final(x1, ys.reshape(-1, ns, lw), pos, top_w, mod3, ln2_g[l], ln2_b[l], F32, alpha=alpha)
    return x.astype(out_dtype)
```
